```python
import math
import jax, jax.numpy as jnp
from jax import lax
import numpy as np

D_MODEL = 1024
BATCH = 8
SEQ = 2048
DEPTH = 1
DEC_BATCH = 128
DEC_SEQ = 1
PAST_LEN = 8192
PAGE_SIZE = 128

D_MIX = D_MODEL
D_ATTN = D_MIX // 2
D_CONV = D_MIX - D_ATTN
HEAD_DIM = 64
N_HEADS = D_ATTN // HEAD_DIM
CONV_GROUP = 64
N_CONV_GROUPS = D_CONV // CONV_GROUP
CONV_WIDTH = 31
MOBA_BLOCK = 256
MOBA_TOPK = 3
N_BUCKETS = 32
MAX_DISTANCE = 128
D_FF = 4 * D_MODEL
Q_CHUNK = 128
EPS = 1e-6
D_IN = 3 * D_ATTN + 2 * D_CONV

kernel_name = "hymba_conformer_moba_decoder_step"


def rms_norm(x, g):
    xf = x.astype(jnp.float32)
    y = xf * lax.rsqrt(jnp.mean(xf * xf, axis=-1, keepdims=True) + EPS)
    return (y * g.astype(jnp.float32)).astype(x.dtype)


def layer_norm(x, g, b):
    xf = x.astype(jnp.float32)
    mu = jnp.mean(xf, axis=-1, keepdims=True)
    xc = xf - mu
    y = xc * lax.rsqrt(jnp.mean(xc * xc, axis=-1, keepdims=True) + EPS)
    return (y * g.astype(jnp.float32) + b.astype(jnp.float32)).astype(x.dtype)


def rel_bucket(dist):
    n = jnp.maximum(dist, 0)
    max_exact = N_BUCKETS // 2
    large = max_exact + (
        jnp.log(jnp.maximum(n, max_exact).astype(jnp.float32) / max_exact)
        / math.log(MAX_DISTANCE / max_exact) * (N_BUCKETS - max_exact)
    ).astype(jnp.int32)
    large = jnp.minimum(large, N_BUCKETS - 1)
    return jnp.where(n < max_exact, n, large)


def moba_prepare(k, v):
    L = k.shape[0]
    nb = -(-L // MOBA_BLOCK)
    pad = nb * MOBA_BLOCK - L
    kb = jnp.pad(k, ((0, pad), (0, 0), (0, 0))).reshape(nb, MOBA_BLOCK, N_HEADS, HEAD_DIM).transpose(2, 0, 1, 3)
    vb = jnp.pad(v, ((0, pad), (0, 0), (0, 0))).reshape(nb, MOBA_BLOCK, N_HEADS, HEAD_DIM).transpose(2, 0, 1, 3)
    kmean = jnp.mean(kb.astype(jnp.float32), axis=2)
    return kb, vb, kmean


def moba_query(q, qpos, kb, vb, kmean, rel_bias):
    tq = q.shape[0]
    nb = kb.shape[1]
    qf = q.astype(jnp.float32)
    qblk = qpos // MOBA_BLOCK
    gate = jnp.einsum('thd,hjd->thj', qf, kmean)
    eligible = jnp.arange(nb)[None, None, :] < qblk[:, None, None]
    gate = jnp.where(eligible, gate, -jnp.inf)
    n_sel = min(MOBA_TOPK, nb)
    _, sel = lax.top_k(gate, n_sel)
    sel_ok = jnp.broadcast_to(jnp.arange(n_sel)[None, None, :] < qblk[:, None, None], sel.shape)
    own = jnp.broadcast_to(qblk[:, None, None], (tq, N_HEADS, 1)).astype(sel.dtype)
    blocks = jnp.concatenate([sel, own], axis=-1)
    block_ok = jnp.concatenate([sel_ok, jnp.ones(own.shape, dtype=bool)], axis=-1)
    hidx = jnp.arange(N_HEADS)[None, :, None]
    kg = kb[hidx, blocks]
    vg = vb[hidx, blocks]
    kpos = blocks[..., None] * MOBA_BLOCK + jnp.arange(MOBA_BLOCK)
    dist = qpos[:, None, None, None] - kpos
    bias = rel_bias[rel_bucket(dist), jnp.arange(N_HEADS)[None, :, None, None]]
    logits = jnp.einsum('thd,thnbd->thnb', qf, kg.astype(jnp.float32)) * (HEAD_DIM ** -0.5) \
        + bias.astype(jnp.float32)
    mask = block_ok[..., None] & (dist >= 0)
    logits = jnp.where(mask, logits, -jnp.inf)
    p = jax.nn.softmax(logits.reshape(tq, N_HEADS, -1), axis=-1).reshape(logits.shape)
    out = jnp.einsum('thnb,thnbd->thd', p, vg.astype(jnp.float32))
    return out.astype(q.dtype)


def moba_prompt(q, k, v, rel_bias):
    s = q.shape[1]
    nc = s // Q_CHUNK
    pos = jnp.arange(s, dtype=jnp.int32).reshape(nc, Q_CHUNK)

    def per_seq(args):
        qs, ks, vs = args
        kb, vb, km = moba_prepare(ks, vs)
        qc = qs.reshape(nc, Q_CHUNK, N_HEADS, HEAD_DIM)
        out = lax.map(lambda a: moba_query(a[0], a[1], kb, vb, km, rel_bias), (qc, pos))
        return out.reshape(s, N_HEADS, HEAD_DIM)

    return lax.map(per_seq, (q, k, v))


def moba_sample(q, k, v, cache_k, cache_v, page_table, rel_bias):
    t = q.shape[1]
    past = page_table.shape[1] * cache_k.shape[1]
    qpos = past + jnp.arange(t, dtype=jnp.int32)

    def per_seq(args):
        qs, ks, vs, pages = args
        kf = jnp.concatenate([cache_k[pages].reshape(past, N_HEADS, HEAD_DIM), ks.astype(cache_k.dtype)], axis=0)
        vf = jnp.concatenate([cache_v[pages].reshape(past, N_HEADS, HEAD_DIM), vs.astype(cache_v.dtype)], axis=0)
        kb, vb, km = moba_prepare(kf, vf)
        return moba_query(qs, qpos, kb, vb, km, rel_bias)

    return lax.map(per_seq, (q, k, v, page_table))


def conv_module(a, g, hist, w_dw, b_dw, ln_g, ln_b):
    u = a * jax.nn.sigmoid(g)
    ext = jnp.concatenate([hist.astype(u.dtype), u], axis=1)
    y = lax.conv_general_dilated(ext, w_dw[:, None, :].astype(ext.dtype), window_strides=(1,),
                                 padding='VALID', dimension_numbers=('NWC', 'WIO', 'NWC'),
                                 feature_group_count=D_CONV) + b_dw
    y = jax.nn.silu(layer_norm(y, ln_g, ln_b))
    return y, ext[:, -(CONV_WIDTH - 1):]


def decoder_layer(x, c, conv_hist, attend, w_ada, b_ada, g_mix, w_in, w_dw, b_dw, ln_g, ln_b,
                  g_attn_out, g_conv_out, w_out, g_ffn, w_ff1, w_ff2):
    n, t, _ = x.shape
    mod = jax.nn.silu(c) @ w_ada + b_ada
    sh1, sc1, gt1, sh2, sc2, gt2 = jnp.split(mod[:, None, :], 6, axis=-1)
    h = rms_norm(x, g_mix) * (1 + sc1) + sh1
    proj = h @ w_in
    q, k, v, a, g = jnp.split(proj, [D_ATTN, 2 * D_ATTN, 3 * D_ATTN, 3 * D_ATTN + D_CONV], axis=-1)
    q = q.reshape(n, t, N_HEADS, HEAD_DIM)
    k = k.reshape(n, t, N_HEADS, HEAD_DIM)
    v = v.reshape(n, t, N_HEADS, HEAD_DIM)
    attn = attend(q, k, v).reshape(n, t, D_ATTN)
    conv, new_hist = conv_module(a, g, conv_hist, w_dw, b_dw, ln_g, ln_b)
    merged = jnp.concatenate([rms_norm(attn, g_attn_out), rms_norm(conv, g_conv_out)], axis=-1)
    x = x + gt1 * (merged @ w_out)
    h2 = rms_norm(x, g_ffn) * (1 + sc2) + sh2
    x = x + gt2 * (jnp.square(jax.nn.relu(h2 @ w_ff1)) @ w_ff2)
    return x, k, v, new_hist


def setup_inputs(seed: int = 0) -> dict:
    key = jax.random.key(seed)
    ks = jax.random.split(key, 32)
    n_pages = PAST_LEN // PAGE_SIZE
    n_used = DEC_BATCH * n_pages
    n_pool = n_used + n_used // 4
    f32 = jnp.float32
    nrm = lambda k, shape, s: jax.random.normal(k, shape, f32) * s
    page_table = jax.random.permutation(ks[0], n_pool)[:n_used].reshape(DEC_BATCH, n_pages).astype(jnp.int32)
    return {
        "x_prompt": nrm(ks[1], (BATCH, SEQ, D_MODEL), 1.0),
        "x_sample": nrm(ks[2], (DEC_BATCH, DEC_SEQ, D_MODEL), 1.0),
        "c_prompt": nrm(ks[3], (BATCH, D_MODEL), 1.0),
        "c_sample": nrm(ks[4], (DEC_BATCH, D_MODEL), 1.0),
        "cache_k": nrm(ks[5], (DEPTH, n_pool, PAGE_SIZE, N_HEADS, HEAD_DIM), 1.0),
        "cache_v": nrm(ks[6], (DEPTH, n_pool, PAGE_SIZE, N_HEADS, HEAD_DIM), 1.0),
        "state_conv": nrm(ks[7], (DEPTH, DEC_BATCH, CONV_WIDTH - 1, D_CONV), 0.5),
        "page_table": page_table,
        "rel_bias": nrm(ks[8], (N_BUCKETS, N_HEADS), 0.5),
        "w_ada": nrm(ks[9], (DEPTH, D_MODEL, 6 * D_MODEL), 0.5 * D_MODEL ** -0.5),
        "b_ada": nrm(ks[10], (DEPTH, 6 * D_MODEL), 0.02),
        "g_mix": 1.0 + nrm(ks[11], (DEPTH, D_MODEL), 0.05),
        "w_in": nrm(ks[12], (DEPTH, D_MODEL, D_IN), D_MODEL ** -0.5),
        "w_dw": nrm(ks[13], (DEPTH, CONV_WIDTH, D_CONV), CONV_WIDTH ** -0.5),
        "b_dw": nrm(ks[14], (DEPTH, D_CONV), 0.02),
        "ln_conv_g": 1.0 + nrm(ks[15], (DEPTH, D_CONV), 0.05),
        "ln_conv_b": nrm(ks[16], (DEPTH, D_CONV), 0.02),
        "g_attn_out": 1.0 + nrm(ks[17], (DEPTH, D_ATTN), 0.05),
        "g_conv_out": 1.0 + nrm(ks[18], (DEPTH, D_CONV), 0.05),
        "w_out": nrm(ks[19], (DEPTH, D_MIX, D_MODEL), D_MIX ** -0.5),
        "g_ffn": 1.0 + nrm(ks[20], (DEPTH, D_MODEL), 0.05),
        "w_ff1": nrm(ks[21], (DEPTH, D_MODEL, D_FF), D_MODEL ** -0.5),
        "w_ff2": nrm(ks[22], (DEPTH, D_FF, D_MODEL), D_FF ** -0.5),
        "g_final": 1.0 + nrm(ks[23], (D_MODEL,), 0.05),
    }


def reference(x_prompt, x_sample, c_prompt, c_sample, cache_k, cache_v, state_conv, page_table,
              rel_bias, w_ada, b_ada, g_mix, w_in, w_dw, b_dw, ln_conv_g, ln_conv_b,
              g_attn_out, g_conv_out, w_out, g_ffn, w_ff1, w_ff2, g_final):
    xp, xs = x_prompt, x_sample
    kp_l, vp_l, cp_l, ks_l, vs_l, cs_l = [], [], [], [], [], []
    for l in range(DEPTH):
        params = (w_ada[l], b_ada[l], g_mix[l], w_in[l], w_dw[l], b_dw[l], ln_conv_g[l], ln_conv_b[l],
                  g_attn_out[l], g_conv_out[l], w_out[l], g_ffn[l], w_ff1[l], w_ff2[l])
        hist0 = jnp.zeros((xp.shape[0], CONV_WIDTH - 1, D_CONV), xp.dtype)
        prompt_attend = lambda q, k, v: moba_prompt(q, k, v, rel_bias)
        sample_attend = lambda q, k, v, l=l: moba_sample(q, k, v, cache_k[l], cache_v[l], page_table, rel_bias)
        xp, kp, vp, cp = decoder_layer(xp, c_prompt, hist0, prompt_attend, *params)
        xs, ksm, vsm, csm = decoder_layer(xs, c_sample, state_conv[l], sample_attend, *params)
        kp_l.append(kp); vp_l.append(vp); cp_l.append(cp)
        ks_l.append(ksm); vs_l.append(vsm); cs_l.append(csm)
    y_prompt = rms_norm(xp, g_final)
    y_sample = rms_norm(xs, g_final)
    return (y_prompt, y_sample, jnp.stack(kp_l), jnp.stack(vp_l), jnp.stack(cp_l),
            jnp.stack(ks_l), jnp.stack(vs_l), jnp.stack(cs_l))
```

```python
import functools
import math

import numpy as np
import jax
import jax.numpy as jnp
from jax import lax
from jax.experimental import pallas as pl
from jax.experimental.pallas import tpu as pltpu

D_MODEL = 1024
BATCH = 8
SEQ = 2048
DEC_BATCH = 128
PAST_LEN = 8192
PAGE_SIZE = 128
D_ATTN = 512
D_CONV = 512
HEAD_DIM = 64
N_HEADS = 8
CONV_WIDTH = 31
MOBA_BLOCK = 256
MOBA_TOPK = 3
N_BUCKETS = 32
MAX_DISTANCE = 128
D_FF = 4096
EPS = 1e-6
D_IN = 3 * D_ATTN + 2 * D_CONV
N_PAGES = PAST_LEN // PAGE_SIZE
N_PAST_BLOCKS = PAST_LEN // MOBA_BLOCK
PAGES_PER_BLOCK = MOBA_BLOCK // PAGE_SIZE
N_PROMPT_BLOCKS = SEQ // MOBA_BLOCK

F32 = jnp.float32
BF16 = jnp.bfloat16
NEG = -1e30
VMEM_LIMIT = 56 * 1024 * 1024


def _bucket_thresholds():
    n = np.arange(0, 4 * MAX_DISTANCE)
    max_exact = N_BUCKETS // 2
    ratio = np.maximum(n, max_exact).astype(np.float32) / np.float32(max_exact)
    val = np.log(ratio) / np.float32(math.log(MAX_DISTANCE / max_exact)) * np.float32(N_BUCKETS - max_exact)
    large = np.minimum(max_exact + val.astype(np.int32), N_BUCKETS - 1)
    bucket = np.where(n < max_exact, n, large)
    assert np.all(np.diff(bucket) >= 0) and bucket[-1] == N_BUCKETS - 1
    return [int(np.argmax(bucket >= b)) for b in range(N_BUCKETS)]


BUCKET_START = _bucket_thresholds()
FAR_DISTANCE = BUCKET_START[N_BUCKETS - 1]
assert FAR_DISTANCE <= MOBA_BLOCK


def _rms(x, g):
    return x * lax.rsqrt(jnp.mean(x * x, axis=-1, keepdims=True) + EPS) * g


def _bias_of_distance(dist, rb_ref, h):
    val = jnp.full(dist.shape, rb_ref[0, h], F32)
    for b in range(1, N_BUCKETS):
        val = jnp.where(dist >= BUCKET_START[b], rb_ref[b, h], val)
    return val


def _mod_kernel(c_ref, w_ref, b_ref, o_ref):
    c = c_ref[...]
    s = (c * jax.nn.sigmoid(c)).astype(BF16)
    o_ref[...] = jnp.dot(s, w_ref[...].astype(BF16), preferred_element_type=F32) + b_ref[...]


def _mod(c_all, w_ada, b_ada):
    rows = c_all.shape[0]
    bn = 1024
    return pl.pallas_call(
        _mod_kernel,
        grid=(6 * D_MODEL // bn,),
        in_specs=[
            pl.BlockSpec((rows, D_MODEL), lambda j: (0, 0)),
            pl.BlockSpec((D_MODEL, bn), lambda j: (0, j)),
            pl.BlockSpec((1, bn), lambda j: (0, j)),
        ],
        out_specs=pl.BlockSpec((rows, bn), lambda j: (0, j)),
        out_shape=jax.ShapeDtypeStruct((rows, 6 * D_MODEL), F32),
        compiler_params=pltpu.CompilerParams(dimension_semantics=("arbitrary",), vmem_limit_bytes=VMEM_LIMIT),
        name="mod",
    )(c_all, w_ada, b_ada)


def _mod_specs_prompt(tm, k):
    tiles_per_seq = SEQ // tm
    return pl.BlockSpec((None, None, 1, D_MODEL), lambda i: (i // tiles_per_seq, k, 0, 0))


def _mod_specs_sample(k):
    return pl.BlockSpec((DEC_BATCH, D_MODEL), lambda i: (0, k))


def _inproj_kernel(x_ref, sh_ref, sc_ref, g_ref, w_ref, q_ref, k_ref, v_ref, u_ref):
    h = _rms(x_ref[...], g_ref[...]) * (1.0 + sc_ref[...]) + sh_ref[...]
    hb = h.astype(BF16)

    def proj(i):
        return jnp.dot(hb, w_ref[:, i * D_ATTN:(i + 1) * D_ATTN], preferred_element_type=F32)

    q_ref[...] = proj(0)
    k_ref[...] = proj(1)
    v_ref[...] = proj(2)
    a = proj(3)
    g = proj(4)
    u_ref[...] = a * jax.nn.sigmoid(g)


def _inproj(x2d, mod, mod_spec, g_mix, w_in_b, tm):
    rows = x2d.shape[0]
    row_spec = lambda width: pl.BlockSpec((tm, width), lambda i: (i, 0))
    out = jax.ShapeDtypeStruct((rows, D_ATTN), F32)
    return pl.pallas_call(
        _inproj_kernel,
        grid=(rows // tm,),
        in_specs=[
            row_spec(D_MODEL),
            mod_spec(0),
            mod_spec(1),
            pl.BlockSpec((1, D_MODEL), lambda i: (0, 0)),
            pl.BlockSpec((D_MODEL, D_IN), lambda i: (0, 0)),
        ],
        out_specs=[row_spec(D_ATTN)] * 4,
        out_shape=[out] * 4,
        compiler_params=pltpu.CompilerParams(dimension_semantics=("arbitrary",), vmem_limit_bytes=VMEM_LIMIT),
        name="inproj",
    )(x2d, mod, mod, g_mix, w_in_b)


CONV_TM = 256
CONV_HALO = 32


def _conv_post(y, lng, lnb, gco):
    mu = jnp.mean(y, axis=-1, keepdims=True)
    yc = y - mu
    yn = yc * lax.rsqrt(jnp.mean(yc * yc, axis=-1, keepdims=True) + EPS) * lng + lnb
    s = yn * jax.nn.sigmoid(yn)
    return _rms(s, gco)


def _conv_kernel(prev_ref, cur_ref, wdw_ref, bdw_ref, lng_ref, lnb_ref, gco_ref, o_ref):
    i = pl.program_id(1)
    prev = jnp.where(i == 0, 0.0, prev_ref[...])
    win = jnp.concatenate([prev, cur_ref[...]], axis=0)
    acc = jnp.zeros((CONV_TM, D_CONV), F32)
    first = CONV_HALO - (CONV_WIDTH - 1)
    for w in range(CONV_WIDTH):
        acc = acc + win[first + w:first + w + CONV_TM, :] * wdw_ref[w:w + 1, :]
    y = acc + bdw_ref[...]
    o_ref[...] = _conv_post(y, lng_ref[...], lnb_ref[...], gco_ref[...]).astype(BF16)


def _conv_prompt(u3, w_dw, b_dw, ln_g, ln_b, g_co):
    halo_per_tile = CONV_TM // CONV_HALO
    vec = pl.BlockSpec((1, D_CONV), lambda b, i: (0, 0))
    return pl.pallas_call(
        _conv_kernel,
        grid=(BATCH, SEQ // CONV_TM),
        in_specs=[
            pl.BlockSpec((None, CONV_HALO, D_CONV), lambda b, i: (b, jnp.maximum(i * halo_per_tile - 1, 0), 0)),
            pl.BlockSpec((None, CONV_TM, D_CONV), lambda b, i: (b, i, 0)),
            pl.BlockSpec((CONV_WIDTH, D_CONV), lambda b, i: (0, 0)),
            vec, vec, vec, vec,
        ],
        out_specs=pl.BlockSpec((None, CONV_TM, D_CONV), lambda b, i: (b, i, 0)),
        out_shape=jax.ShapeDtypeStruct((BATCH, SEQ, D_CONV), BF16),
        compiler_params=pltpu.CompilerParams(dimension_semantics=("arbitrary", "arbitrary"),
                                             vmem_limit_bytes=VMEM_LIMIT),
        name="conv_prompt",
    )(u3, u3, w_dw, b_dw, ln_g, ln_b, g_co)


SCONV_TN = 32


def _sconv_kernel(hist_ref, u_ref, wdw_ref, bdw_ref, lng_ref, lnb_ref, gco_ref, o_ref):
    hist = hist_ref[...]
    u = u_ref[...]
    w_hist = wdw_ref[0:CONV_WIDTH - 1, :]
    y = jnp.sum(hist * w_hist[None], axis=1) + u * wdw_ref[CONV_WIDTH - 1:CONV_WIDTH, :] + bdw_ref[...]
    o_ref[...] = _conv_post(y, lng_ref[...], lnb_ref[...], gco_ref[...]).astype(BF16)


def _conv_sample(hist, u, w_dw, b_dw, ln_g, ln_b, g_co):
    vec = pl.BlockSpec((1, D_CONV), lambda i: (0, 0))
    return pl.pallas_call(
        _sconv_kernel,
        grid=(DEC_BATCH // SCONV_TN,),
        in_specs=[
            pl.BlockSpec((SCONV_TN, CONV_WIDTH - 1, D_CONV), lambda i: (i, 0, 0)),
            pl.BlockSpec((SCONV_TN, D_CONV), lambda i: (i, 0)),
            pl.BlockSpec((CONV_WIDTH, D_CONV), lambda i: (0, 0)),
            vec, vec, vec, vec,
        ],
        out_specs=pl.BlockSpec((SCONV_TN, D_CONV), lambda i: (i, 0)),
        out_shape=jax.ShapeDtypeStruct((DEC_BATCH, D_CONV), BF16),
        compiler_params=pltpu.CompilerParams(dimension_semantics=("arbitrary",), vmem_limit_bytes=VMEM_LIMIT),
        name="conv_sample",
    )(hist, u, w_dw, b_dw, ln_g, ln_b, g_co)


PAIR = 2 * HEAD_DIM
NT_DIMS = (((1,), (1,)), ((), ()))


def _moba_kernel(rb_ref, q_ref, k_ref, v_ref, o_ref, kb_ref, vt_ref, km_ref, bias_ref, mask_ref):
    b = pl.program_id(0)
    hp = pl.program_id(1)
    qi = pl.program_id(2)
    blk = MOBA_BLOCK

    @pl.when(qi == 0)
    def _():
        for j in range(N_PROMPT_BLOCKS):
            kj = k_ref[j * blk:(j + 1) * blk, :]
            kb_ref[j] = kj.astype(BF16)
            km_ref[j:j + 1, :] = jnp.sum(kj, axis=0, keepdims=True) * (1.0 / blk)
            vt_ref[j] = v_ref[j * blk:(j + 1) * blk, :].T.astype(BF16)

    @pl.when((b == 0) & (qi == 0))
    def _():
        kk = lax.broadcasted_iota(jnp.int32, (blk, blk), 0)
        qq = lax.broadcasted_iota(jnp.int32, (blk, blk), 1)
        d0 = qq - kk
        for e in range(2):
            h = 2 * hp + e
            bias_ref[h, 0] = jnp.where(d0 >= 0, _bias_of_distance(jnp.maximum(d0, 0), rb_ref, h), NEG)
            bias_ref[h, 1] = _bias_of_distance(d0 + blk, rb_ref, h)

    q = q_ref[...]
    lane = lax.broadcasted_iota(jnp.int32, (1, PAIR), 1)
    row8 = lax.broadcasted_iota(jnp.int32, (N_PROMPT_BLOCKS, blk), 0)
    out_t = None
    for e in range(2):
        h = 2 * hp + e
        head_lanes = (lane >= e * HEAD_DIM) & (lane < (e + 1) * HEAD_DIM)
        qm = jnp.where(head_lanes, q, 0.0)

        gate = lax.dot_general(km_ref[...], qm, NT_DIMS, precision=lax.Precision.HIGHEST,
                               preferred_element_type=F32)
        cnt = jnp.zeros(gate.shape, jnp.int32)
        for i in range(N_PROMPT_BLOCKS):
            gi = gate[i:i + 1, :]
            beats = jnp.where(gi > gate, 1, jnp.where((gi == gate) & (i < row8), 1, 0))
            cnt = cnt + jnp.where(i < qi, beats, 0)
        keep = jnp.where(row8 < qi, jnp.where(cnt < MOBA_TOPK, 0.0, NEG), NEG)
        mask_ref[e] = keep

        qb = (qm * (HEAD_DIM ** -0.5)).astype(BF16)
        far_bias = rb_ref[N_BUCKETS - 1, h]

        s = lax.dot_general(kb_ref[qi], qb, NT_DIMS, preferred_element_type=F32) + bias_ref[h, 0]
        m = jnp.max(s, axis=0, keepdims=True)
        p = jnp.exp(s - m)
        l = jnp.sum(p, axis=0, keepdims=True)
        acc = jnp.dot(vt_ref[qi], p.astype(BF16), preferred_element_type=F32)

        def body(j, carry, e=e, h=h, qb=qb, far_bias=far_bias):
            m, l, acc = carry
            s = lax.dot_general(kb_ref[j], qb, NT_DIMS, preferred_element_type=F32)
            s = s + jnp.where(j == qi - 1, bias_ref[h, 1], far_bias) + mask_ref[e, pl.ds(j, 1), :]
            m_new = jnp.maximum(m, jnp.max(s, axis=0, keepdims=True))
            alpha = jnp.exp(m - m_new)
            p = jnp.exp(s - m_new)
            l = alpha * l + jnp.sum(p, axis=0, keepdims=True)
            acc = alpha * acc + jnp.dot(vt_ref[j], p.astype(BF16), preferred_element_type=F32)
            return m_new, l, acc

        m, l, acc = lax.fori_loop(0, qi, body, (m, l, acc))
        res = acc / l
        if e == 0:
            out_t = res
        else:
            drow = lax.broadcasted_iota(jnp.int32, (PAIR, 1), 0)
            out_t = jnp.where(drow < HEAD_DIM, out_t, res)
    o_ref[...] = out_t.T


def _moba_prompt(rel_bias, q3, k3, v3):
    nq = SEQ // MOBA_BLOCK
    return pl.pallas_call(
        _moba_kernel,
        grid=(BATCH, N_HEADS // 2, nq),
        in_specs=[
            pl.BlockSpec(memory_space=pltpu.SMEM),
            pl.BlockSpec((None, MOBA_BLOCK, PAIR), lambda b, hp, qi: (b, qi, hp)),
            pl.BlockSpec((None, SEQ, PAIR), lambda b, hp, qi: (b, 0, hp)),
            pl.BlockSpec((None, SEQ, PAIR), lambda b, hp, qi: (b, 0, hp)),
        ],
        out_specs=pl.BlockSpec((None, MOBA_BLOCK, PAIR), lambda b, hp, qi: (b, qi, hp)),
        out_shape=jax.ShapeDtypeStruct((BATCH, SEQ, D_ATTN), F32),
        scratch_shapes=[
            pltpu.VMEM((N_PROMPT_BLOCKS, MOBA_BLOCK, PAIR), BF16),
            pltpu.VMEM((N_PROMPT_BLOCKS, PAIR, MOBA_BLOCK), BF16),
            pltpu.VMEM((N_PROMPT_BLOCKS, PAIR), F32),
            pltpu.VMEM((N_HEADS, 2, MOBA_BLOCK, MOBA_BLOCK), F32),
            pltpu.VMEM((2, N_PROMPT_BLOCKS, MOBA_BLOCK), F32),
        ],
        compiler_params=pltpu.CompilerParams(dimension_semantics=("arbitrary", "arbitrary", "arbitrary"),
                                             vmem_limit_bytes=VMEM_LIMIT),
        name="moba_prompt",
    )(rel_bias, q3, k3, v3)


GATE_PAGES = 16
GATE_STEPS = N_PAGES // GATE_PAGES
GATE_BLOCKS = GATE_PAGES // PAGES_PER_BLOCK


def _gate_kernel(pt_ref, q_ref, *refs):
    del pt_ref
    pages = refs[:GATE_PAGES]
    idx_ref = refs[GATE_PAGES]
    g_ref = refs[GATE_PAGES + 1]
    c = pl.program_id(1)
    q = q_ref[...]
    lane = lax.broadcasted_iota(jnp.int32, (N_HEADS, 128), 1)

    @pl.when(c == 0)
    def _():
        g_ref[...] = jnp.full((N_HEADS, 128), -jnp.inf, F32)

    g = g_ref[...]
    for bi in range(GATE_BLOCKS):
        ksum = jnp.sum(pages[2 * bi][...], axis=0) + jnp.sum(pages[2 * bi + 1][...], axis=0)
        kmean = ksum * (1.0 / MOBA_BLOCK)
        gcol = jnp.sum(kmean * q, axis=1, keepdims=True)
        g = jnp.where(lane == c * GATE_BLOCKS + bi, gcol, g)
    g_ref[...] = g

    @pl.when(c == GATE_STEPS - 1)
    def _():
        lane_f = lane.astype(F32)
        gg = g
        out = jnp.zeros((N_HEADS, 128), jnp.int32)
        for r in range(MOBA_TOPK):
            best = jnp.max(gg, axis=1, keepdims=True)
            pick = jnp.min(jnp.where(gg == best, lane_f, 128.0), axis=1, keepdims=True)
            out = jnp.where(lane == r, pick.astype(jnp.int32), out)
            gg = jnp.where(lane_f == pick, -jnp.inf, gg)
        idx_ref[...] = out


def _gate_sample(pt_flat, q3, cache_k):
    def page_spec(i):
        return pl.BlockSpec((None, None, PAGE_SIZE, N_HEADS, HEAD_DIM),
                            lambda n, c, pt: (0, pt[n * N_PAGES + c * GATE_PAGES + i], 0, 0, 0))

    grid_spec = pltpu.PrefetchScalarGridSpec(
        num_scalar_prefetch=1,
        grid=(DEC_BATCH, GATE_STEPS),
        in_specs=[pl.BlockSpec((None, N_HEADS, HEAD_DIM), lambda n, c, pt: (n, 0, 0))]
        + [page_spec(i) for i in range(GATE_PAGES)],
        out_specs=pl.BlockSpec((None, N_HEADS, 128), lambda n, c, pt: (n, 0, 0)),
        scratch_shapes=[pltpu.VMEM((N_HEADS, 128), F32)],
    )
    return pl.pallas_call(
        _gate_kernel,
        grid_spec=grid_spec,
        out_shape=jax.ShapeDtypeStruct((DEC_BATCH, N_HEADS, 128), jnp.int32),
        compiler_params=pltpu.CompilerParams(dimension_semantics=("arbitrary", "arbitrary"),
                                             vmem_limit_bytes=VMEM_LIMIT),
        name="gate_sample",
    )(pt_flat, q3, *([cache_k] * GATE_PAGES))


SEL_ROWS = MOBA_TOPK * MOBA_BLOCK
SLABS = MOBA_TOPK * PAGES_PER_BLOCK


def _sattn_copies(sel_ref, pt_ref, ck_hbm, cv_hbm, kbuf, vbuf, sem, n, slot):
    out = []
    for h in range(N_HEADS):
        for s in range(MOBA_TOPK):
            blk = sel_ref[(n * N_HEADS + h) * MOBA_TOPK + s]
            for half in range(PAGES_PER_BLOCK):
                page = pt_ref[n * N_PAGES + blk * PAGES_PER_BLOCK + half]
                rows = pl.ds((s * PAGES_PER_BLOCK + half) * PAGE_SIZE, PAGE_SIZE)
                out.append(pltpu.make_async_copy(ck_hbm.at[0, page, :, h, :], kbuf.at[slot, h, rows, :], sem.at[0, slot]))
                out.append(pltpu.make_async_copy(cv_hbm.at[0, page, :, h, :], vbuf.at[slot, h, rows, :], sem.at[1, slot]))
    return out


def _sattn_kernel(sel_ref, pt_ref, rb_ref, q_ref, kn_ref, vn_ref, ck_hbm, cv_hbm, o_ref, kbuf, vbuf, tab_ref, sem):
    n = pl.program_id(0)
    slot = lax.rem(n, 2)
    copies = functools.partial(_sattn_copies, sel_ref, pt_ref, ck_hbm, cv_hbm, kbuf, vbuf, sem)

    @pl.when(n == 0)
    def _():
        for c in copies(0, 0):
            c.start()
        dist = MOBA_BLOCK - lax.broadcasted_iota(jnp.int32, (1, MOBA_BLOCK), 1)
        for h in range(N_HEADS):
            tab_ref[h:h + 1, :] = _bias_of_distance(dist, rb_ref, h)

    @pl.when(n + 1 < DEC_BATCH)
    def _():
        for c in copies(n + 1, 1 - slot):
            c.start()

    for c in copies(n, slot):
        c.wait()

    q = q_ref[...]
    scale = HEAD_DIM ** -0.5
    for h in range(N_HEADS):
        kh = kbuf[slot, h]
        vh = vbuf[slot, h]
        qh = q[h:h + 1, :]
        s = lax.dot_general(q, kh, NT_DIMS, preferred_element_type=F32)[h:h + 1, :] * scale
        far_bias = rb_ref[N_BUCKETS - 1, h]
        bias = []
        for t in range(MOBA_TOPK):
            blk = sel_ref[(n * N_HEADS + h) * MOBA_TOPK + t]
            bias.append(jnp.where(blk == N_PAST_BLOCKS - 1, tab_ref[h:h + 1, :], far_bias))
        s = s + jnp.concatenate(bias, axis=1)
        s_new = jnp.sum(qh * kn_ref[h:h + 1, :], axis=1, keepdims=True) * scale + rb_ref[0, h]
        m = jnp.maximum(jnp.max(s, axis=1, keepdims=True), s_new)
        p = jnp.exp(s - m)
        p_new = jnp.exp(s_new - m)
        l = jnp.sum(p, axis=1, keepdims=True) + p_new
        pv = jnp.dot(jnp.broadcast_to(p, (N_HEADS, SEL_ROWS)), vh, preferred_element_type=F32)[h:h + 1, :]
        o_ref[h:h + 1, :] = (pv + p_new * vn_ref[h:h + 1, :]) / l


def _attn_sample(sel_flat, pt_flat, rel_bias, q3, k3, v3, cache_k, cache_v):
    head_spec = pl.BlockSpec((None, N_HEADS, HEAD_DIM), lambda n, sel, pt: (n, 0, 0))
    grid_spec = pltpu.PrefetchScalarGridSpec(
        num_scalar_prefetch=2,
        grid=(DEC_BATCH,),
        in_specs=[
            pl.BlockSpec(memory_space=pltpu.SMEM),
            head_spec, head_spec, head_spec,
            pl.BlockSpec(memory_space=pl.ANY),
            pl.BlockSpec(memory_space=pl.ANY),
        ],
        out_specs=head_spec,
        scratch_shapes=[
            pltpu.VMEM((2, N_HEADS, SEL_ROWS, HEAD_DIM), F32),
            pltpu.VMEM((2, N_HEADS, SEL_ROWS, HEAD_DIM), F32),
            pltpu.VMEM((N_HEADS, MOBA_BLOCK), F32),
            pltpu.SemaphoreType.DMA((2, 2)),
        ],
    )
    return pl.pallas_call(
        _sattn_kernel,
        grid_spec=grid_spec,
        out_shape=jax.ShapeDtypeStruct((DEC_BATCH, N_HEADS, HEAD_DIM), F32),
        compiler_params=pltpu.CompilerParams(dimension_semantics=("arbitrary",), vmem_limit_bytes=VMEM_LIMIT),
        name="attn_sample",
    )(sel_flat, pt_flat, rel_bias, q3, k3, v3, cache_k, cache_v)


FF_CHUNK = 1024


def _ffn_kernel(x_ref, attn_ref, conv_ref, gt1_ref, sh2_ref, sc2_ref, gt2_ref, ga_ref, gffn_ref, gfin_ref,
                wo_ref, w1_ref, w2_ref, y_ref):
    attn_n = _rms(attn_ref[...], ga_ref[...]).astype(BF16)
    mixed = (jnp.dot(attn_n, wo_ref[:D_ATTN, :], preferred_element_type=F32)
             + jnp.dot(conv_ref[...], wo_ref[D_ATTN:, :], preferred_element_type=F32))
    x1 = x_ref[...] + gt1_ref[...] * mixed
    h2 = (_rms(x1, gffn_ref[...]) * (1.0 + sc2_ref[...]) + sh2_ref[...]).astype(BF16)
    acc = jnp.zeros(x1.shape, F32)
    for c in range(D_FF // FF_CHUNK):
        f = jnp.dot(h2, w1_ref[:, c * FF_CHUNK:(c + 1) * FF_CHUNK], preferred_element_type=F32)
        f = jnp.square(jnp.maximum(f, 0.0)).astype(BF16)
        acc = acc + jnp.dot(f, w2_ref[c * FF_CHUNK:(c + 1) * FF_CHUNK, :], preferred_element_type=F32)
    x2 = x1 + gt2_ref[...] * acc
    y_ref[...] = _rms(x2, gfin_ref[...])


def _ffn(x2d, attn2d, conv2d, mod, mod_spec, g_attn, g_ffn, g_final, wo_b, w1_b, w2_b, tm):
    rows = x2d.shape[0]
    row_spec = lambda width: pl.BlockSpec((tm, width), lambda i: (i, 0))
    const = lambda shape: pl.BlockSpec(shape, lambda i: (0, 0), pipeline_mode=pl.Buffered(1))
    return pl.pallas_call(
        _ffn_kernel,
        grid=(rows // tm,),
        in_specs=[
            row_spec(D_MODEL), row_spec(D_ATTN), row_spec(D_CONV),
            mod_spec(2), mod_spec(3), mod_spec(4), mod_spec(5),
            const((1, D_ATTN)), const((1, D_MODEL)), const((1, D_MODEL)),
            const((D_MODEL, D_MODEL)), const((D_MODEL, D_FF)), const((D_FF, D_MODEL)),
        ],
        out_specs=row_spec(D_MODEL),
        out_shape=jax.ShapeDtypeStruct((rows, D_MODEL), F32),
        compiler_params=pltpu.CompilerParams(dimension_semantics=("arbitrary",), vmem_limit_bytes=VMEM_LIMIT),
        name="ffn",
    )(x2d, attn2d, conv2d, mod, mod, mod, mod, g_attn, g_ffn, g_final, wo_b, w1_b, w2_b)


def kernel(x_prompt, x_sample, c_prompt, c_sample, cache_k, cache_v, state_conv, page_table, rel_bias, w_ada, b_ada, g_mix, w_in, w_dw, b_dw, ln_conv_g, ln_conv_b, g_attn_out, g_conv_out, w_out, g_ffn, w_ff1, w_ff2, g_final):
    w_in_b = w_in[0].astype(BF16)
    wo_b = w_out[0].astype(BF16)
    w1_b = w_ff1[0].astype(BF16)
    w2_b = w_ff2[0].astype(BF16)
    g_fin = g_final.reshape(1, D_MODEL)

    mod = _mod(jnp.concatenate([c_prompt, c_sample], axis=0), w_ada[0], b_ada)
    mod_p = mod[:BATCH].reshape(BATCH, 6, 1, D_MODEL)
    mod_s = mod[BATCH:]

    tm = 512
    xp = x_prompt.reshape(BATCH * SEQ, D_MODEL)
    q, k, v, u = _inproj(xp, mod_p, functools.partial(_mod_specs_prompt, tm), g_mix, w_in_b, tm)
    seq3 = lambda a: a.reshape(BATCH, SEQ, a.shape[-1])
    conv_n = _conv_prompt(seq3(u), w_dw[0], b_dw, ln_conv_g, ln_conv_b, g_conv_out)
    attn = _moba_prompt(rel_bias, seq3(q), seq3(k), seq3(v))
    y_p = _ffn(xp, attn.reshape(BATCH * SEQ, D_ATTN), conv_n.reshape(BATCH * SEQ, D_CONV), mod_p,
               functools.partial(_mod_specs_prompt, tm), g_attn_out, g_ffn, g_fin, wo_b, w1_b, w2_b, tm)

    xs = x_sample.reshape(DEC_BATCH, D_MODEL)
    qs, ks, vs, us = _inproj(xs, mod_s, _mod_specs_sample, g_mix, w_in_b, DEC_BATCH)
    conv_s = _conv_sample(state_conv[0], us, w_dw[0], b_dw, ln_conv_g, ln_conv_b, g_conv_out)
    heads = lambda a: a.reshape(DEC_BATCH, N_HEADS, HEAD_DIM)
    pt_flat = page_table.reshape(-1)
    sel = _gate_sample(pt_flat, heads(qs), cache_k)
    sel_flat = sel[:, :, :MOBA_TOPK].reshape(-1)
    attn_s = _attn_sample(sel_flat, pt_flat, rel_bias, heads(qs), heads(ks), heads(vs), cache_k, cache_v)
    y_s = _ffn(xs, attn_s.reshape(DEC_BATCH, D_ATTN), conv_s, mod_s, _mod_specs_sample,
               g_attn_out, g_ffn, g_fin, wo_b, w1_b, w2_b, DEC_BATCH)

    kv5 = lambda a, n, t: a.reshape(1, n, t, N_HEADS, HEAD_DIM)
    hist_p = seq3(u)[:, SEQ - (CONV_WIDTH - 1):, :][None]
    hist_s = jnp.concatenate([state_conv[0][:, 1:, :], us[:, None, :]], axis=1)[None]
    return (y_p.reshape(BATCH, SEQ, D_MODEL), y_s.reshape(DEC_BATCH, 1, D_MODEL),
            kv5(k, BATCH, SEQ), kv5(v, BATCH, SEQ), hist_p,
            kv5(ks, DEC_BATCH, 1), kv5(vs, DEC_BATCH, 1), hist_s)
```

```python
import functools
import math

import numpy as np
import jax
import jax.numpy as jnp
from jax import lax
from jax.experimental import pallas as pl
from jax.experimental.pallas import tpu as pltpu

D_MODEL = 1024
BATCH = 8
SEQ = 2048
DEC_BATCH = 128
PAST_LEN = 8192
PAGE_SIZE = 128
D_ATTN = 512
D_CONV = 512
HEAD_DIM = 64
N_HEADS = 8
CONV_WIDTH = 31
MOBA_BLOCK = 256
MOBA_TOPK = 3
N_BUCKETS = 32
MAX_DISTANCE = 128
D_FF = 4096
EPS = 1e-6
D_IN = 3 * D_ATTN + 2 * D_CONV
N_PAGES = PAST_LEN // PAGE_SIZE
N_PAST_BLOCKS = PAST_LEN // MOBA_BLOCK
PAGES_PER_BLOCK = MOBA_BLOCK // PAGE_SIZE
N_PROMPT_BLOCKS = SEQ // MOBA_BLOCK
LANES = 128
SUBLANES = 8

F32 = jnp.float32
BF16 = jnp.bfloat16
NEG = -1e30
LOG2E = math.log2(math.e)
VMEM_LIMIT = 56 * 1024 * 1024


def _bucket_thresholds():
    n = np.arange(0, 4 * MAX_DISTANCE)
    max_exact = N_BUCKETS // 2
    ratio = np.maximum(n, max_exact).astype(np.float32) / np.float32(max_exact)
    val = np.log(ratio) / np.float32(math.log(MAX_DISTANCE / max_exact)) * np.float32(N_BUCKETS - max_exact)
    large = np.minimum(max_exact + val.astype(np.int32), N_BUCKETS - 1)
    bucket = np.where(n < max_exact, n, large)
    assert np.all(np.diff(bucket) >= 0) and bucket[-1] == N_BUCKETS - 1
    return [int(np.argmax(bucket >= b)) for b in range(N_BUCKETS)]


BUCKET_START = _bucket_thresholds()
FAR_DISTANCE = BUCKET_START[N_BUCKETS - 1]
assert FAR_DISTANCE <= MOBA_BLOCK


def _rms(x, g):
    return x * lax.rsqrt(jnp.mean(x * x, axis=-1, keepdims=True) + EPS) * g


def _bias_of_distance(dist, rb_ref, h):
    val = jnp.full(dist.shape, rb_ref[0, h], F32)
    for b in range(1, N_BUCKETS):
        val = jnp.where(dist >= BUCKET_START[b], rb_ref[b, h], val)
    return val


def _mod_kernel(c_ref, w_ref, b_ref, o_ref):
    c = c_ref[...]
    s = (c * jax.nn.sigmoid(c)).astype(BF16)
    o_ref[...] = jnp.dot(s, w_ref[...].astype(BF16), preferred_element_type=F32) + b_ref[...]


def _mod(c_all, w_ada, b_ada):
    rows = c_all.shape[0]
    bn = 1024
    return pl.pallas_call(
        _mod_kernel,
        grid=(6 * D_MODEL // bn,),
        in_specs=[
            pl.BlockSpec((rows, D_MODEL), lambda j: (0, 0)),
            pl.BlockSpec((D_MODEL, bn), lambda j: (0, j)),
            pl.BlockSpec((1, bn), lambda j: (0, j)),
        ],
        out_specs=pl.BlockSpec((rows, bn), lambda j: (0, j)),
        out_shape=jax.ShapeDtypeStruct((rows, 6 * D_MODEL), F32),
        compiler_params=pltpu.CompilerParams(dimension_semantics=("arbitrary",), vmem_limit_bytes=VMEM_LIMIT),
        name="mod",
    )(c_all, w_ada, b_ada)


def _mod_specs_prompt(tm, k):
    tiles_per_seq = SEQ // tm
    return pl.BlockSpec((None, None, 1, D_MODEL), lambda i: (i // tiles_per_seq, k, 0, 0))


def _mod_specs_sample(k):
    return pl.BlockSpec((DEC_BATCH, D_MODEL), lambda i: (0, k))


def _inproj_kernel(x_ref, sh_ref, sc_ref, g_ref, w_ref, qt_ref, kt_ref, vt_ref, u_ref):
    h = _rms(x_ref[...], g_ref[...]) * (1.0 + sc_ref[...]) + sh_ref[...]
    hb = h.astype(BF16)

    def proj(i):
        return jnp.dot(hb, w_ref[:, i * D_ATTN:(i + 1) * D_ATTN], preferred_element_type=F32)

    qt_ref[...] = proj(0).T
    kt_ref[...] = proj(1).T
    vt_ref[...] = proj(2).T
    a = proj(3)
    g = proj(4)
    u_ref[...] = a * jax.nn.sigmoid(g)


def _inproj(x2d, mod, mod_spec, g_mix, w_in_b, tm, t_shape, t_spec):
    rows = x2d.shape[0]
    row_spec = lambda width: pl.BlockSpec((tm, width), lambda i: (i, 0))
    t_out = jax.ShapeDtypeStruct(t_shape, F32)
    return pl.pallas_call(
        _inproj_kernel,
        grid=(rows // tm,),
        in_specs=[
            row_spec(D_MODEL),
            mod_spec(0),
            mod_spec(1),
            pl.BlockSpec((1, D_MODEL), lambda i: (0, 0)),
            pl.BlockSpec((D_MODEL, D_IN), lambda i: (0, 0)),
        ],
        out_specs=[t_spec, t_spec, t_spec, row_spec(D_CONV)],
        out_shape=[t_out, t_out, t_out, jax.ShapeDtypeStruct((rows, D_CONV), F32)],
        compiler_params=pltpu.CompilerParams(dimension_semantics=("arbitrary",), vmem_limit_bytes=VMEM_LIMIT),
        name="inproj",
    )(x2d, mod, mod, g_mix, w_in_b)


CONV_TM = 256
CONV_HALO = 32


def _conv_post(y, lng, lnb, gco):
    mu = jnp.mean(y, axis=-1, keepdims=True)
    yc = y - mu
    yn = yc * lax.rsqrt(jnp.mean(yc * yc, axis=-1, keepdims=True) + EPS) * lng + lnb
    s = yn * jax.nn.sigmoid(yn)
    return _rms(s, gco)


CONV_SPAN = CONV_HALO + CONV_TM - SUBLANES


def _conv_kernel(prev_ref, cur_ref, wdw_ref, bdw_ref, lng_ref, lnb_ref, gco_ref, o_ref, sh_ref):
    i = pl.program_id(1)
    prev = jnp.where(i == 0, 0.0, prev_ref[...])
    win = jnp.concatenate([prev, cur_ref[...]], axis=0)
    first = CONV_HALO - (CONV_WIDTH - 1)
    acc = jnp.zeros((CONV_TM, D_CONV), F32)
    for r in range(SUBLANES):
        offs = [o for o in range(first, first + CONV_WIDTH) if o % SUBLANES == r]
        if r > 0:
            sh_ref[r - 1] = win[r:r + CONV_SPAN, :]
        for o in offs:
            a8 = o - r
            tap = win[a8:a8 + CONV_TM, :] if r == 0 else sh_ref[r - 1, a8:a8 + CONV_TM, :]
            acc = acc + tap * wdw_ref[o - first:o - first + 1, :]
    y = acc + bdw_ref[...]
    o_ref[...] = _conv_post(y, lng_ref[...], lnb_ref[...], gco_ref[...]).astype(BF16)


def _conv_prompt(u3, w_dw, b_dw, ln_g, ln_b, g_co):
    halo_per_tile = CONV_TM // CONV_HALO
    vec = pl.BlockSpec((1, D_CONV), lambda b, i: (0, 0))
    return pl.pallas_call(
        _conv_kernel,
        grid=(BATCH, SEQ // CONV_TM),
        in_specs=[
            pl.BlockSpec((None, CONV_HALO, D_CONV), lambda b, i: (b, jnp.maximum(i * halo_per_tile - 1, 0), 0)),
            pl.BlockSpec((None, CONV_TM, D_CONV), lambda b, i: (b, i, 0)),
            pl.BlockSpec((CONV_WIDTH, D_CONV), lambda b, i: (0, 0)),
            vec, vec, vec, vec,
        ],
        out_specs=pl.BlockSpec((None, CONV_TM, D_CONV), lambda b, i: (b, i, 0)),
        out_shape=jax.ShapeDtypeStruct((BATCH, SEQ, D_CONV), BF16),
        scratch_shapes=[pltpu.VMEM((SUBLANES - 1, CONV_SPAN, D_CONV), F32)],
        compiler_params=pltpu.CompilerParams(dimension_semantics=("arbitrary", "arbitrary"),
                                             vmem_limit_bytes=VMEM_LIMIT),
        name="conv_prompt",
    )(u3, u3, w_dw, b_dw, ln_g, ln_b, g_co)


SCONV_TN = 32
N_HIST = CONV_WIDTH - 1


def _sconv_kernel(hist_ref, u_ref, wdw_ref, bdw_ref, lng_ref, lnb_ref, gco_ref, o_ref, nh_ref):
    u = u_ref[...]
    y = u * wdw_ref[N_HIST:CONV_WIDTH, :] + bdw_ref[...]
    for w in range(N_HIST):
        y = y + hist_ref[w] * wdw_ref[w:w + 1, :]
    o_ref[...] = _conv_post(y, lng_ref[...], lnb_ref[...], gco_ref[...]).astype(BF16)
    for w in range(N_HIST - 1):
        nh_ref[w] = hist_ref[w + 1]
    nh_ref[N_HIST - 1] = u


def _conv_sample(hist_t, u, w_dw, b_dw, ln_g, ln_b, g_co):
    vec = pl.BlockSpec((1, D_CONV), lambda i: (0, 0))
    hist_spec = pl.BlockSpec((N_HIST, SCONV_TN, D_CONV), lambda i: (0, i, 0))
    return pl.pallas_call(
        _sconv_kernel,
        grid=(DEC_BATCH // SCONV_TN,),
        in_specs=[
            hist_spec,
            pl.BlockSpec((SCONV_TN, D_CONV), lambda i: (i, 0)),
            pl.BlockSpec((CONV_WIDTH, D_CONV), lambda i: (0, 0)),
            vec, vec, vec, vec,
        ],
        out_specs=[pl.BlockSpec((SCONV_TN, D_CONV), lambda i: (i, 0)), hist_spec],
        out_shape=[jax.ShapeDtypeStruct((DEC_BATCH, D_CONV), BF16),
                   jax.ShapeDtypeStruct((N_HIST, DEC_BATCH, D_CONV), F32)],
        compiler_params=pltpu.CompilerParams(dimension_semantics=("arbitrary",), vmem_limit_bytes=VMEM_LIMIT),
        name="conv_sample",
    )(hist_t, u, w_dw, b_dw, ln_g, ln_b, g_co)


PAIR = 2 * HEAD_DIM


def _moba_kernel(rb_ref, qt_ref, kt_ref, vt_ref, o_ref, kb_ref, vte_ref, mask_ref, bias_ref, s_ref, p_ref, ot_ref):
    b = pl.program_id(0)
    hp = pl.program_id(1)
    blk = MOBA_BLOCK
    nb = N_PROMPT_BLOCKS
    drow = lax.broadcasted_iota(jnp.int32, (PAIR, 1), 0)
    head_rows = [(drow >= e * HEAD_DIM) & (drow < (e + 1) * HEAD_DIM) for e in range(2)]

    @pl.when(b == 0)
    def _():
        kk = lax.broadcasted_iota(jnp.int32, (blk, blk), 0)
        qq = lax.broadcasted_iota(jnp.int32, (blk, blk), 1)
        d0 = qq - kk
        for e in range(2):
            h = 2 * hp + e
            bias_ref[h, 0] = jnp.where(d0 >= 0, _bias_of_distance(jnp.maximum(d0, 0), rb_ref, h) * LOG2E, NEG)
            bias_ref[h, 1] = _bias_of_distance(d0 + blk, rb_ref, h) * LOG2E

    km_cols = []
    for j in range(nb):
        ktj = kt_ref[:, j * blk:(j + 1) * blk]
        kb_ref[j * blk:(j + 1) * blk, :] = ktj.T.astype(BF16)
        km_cols.append(jnp.sum(ktj, axis=1, keepdims=True) * (1.0 / blk))
    vt = vt_ref[...]
    for e in range(2):
        vte_ref[e] = jnp.where(head_rows[e], vt, 1.0).astype(BF16)

    qt = qt_ref[...]
    qblk = lax.broadcasted_iota(jnp.int32, (1, SEQ), 1) // blk
    for e in range(2):
        h = 2 * hp + e
        far2 = rb_ref[N_BUCKETS - 1, h] * LOG2E
        qh = qt[e * HEAD_DIM:(e + 1) * HEAD_DIM, :]
        gates = [jnp.sum(qh * km_cols[j][e * HEAD_DIM:(e + 1) * HEAD_DIM, :], axis=0, keepdims=True)
                 for j in range(nb - 1)]
        for j in range(nb - 1):
            cnt = jnp.zeros((1, SEQ), jnp.int32)
            for i in range(nb - 1):
                if i == j:
                    continue
                beats = (gates[i] > gates[j]) if i > j else (gates[i] >= gates[j])
                cnt = cnt + jnp.where(beats, jnp.where(i < qblk, 1, 0), 0)
            keep = jnp.where(j < qblk, jnp.where(cnt < MOBA_TOPK, 0.0, NEG), NEG)
            mask_ref[e, j:j + 1, :] = keep + far2

    qscale = (HEAD_DIM ** -0.5) * LOG2E
    for qi in range(nb):
        qs = slice(qi * blk, (qi + 1) * blk)
        keys = (qi + 1) * blk
        for e in range(2):
            h = 2 * hp + e
            far2 = rb_ref[N_BUCKETS - 1, h] * LOG2E
            qtm = jnp.where(head_rows[e], qt_ref[:, qs] * qscale, 0.0).astype(BF16)
            s_ref[0:keys, :] = jnp.dot(kb_ref[0:keys, :], qtm, preferred_element_type=F32)
            rows = []
            tops = []
            for j in range(qi + 1):
                ks = slice(j * blk, (j + 1) * blk)
                if j == qi:
                    sj = s_ref[ks, :] + bias_ref[h, 0]
                    s_ref[ks, :] = sj
                    row = None
                elif j == qi - 1:
                    sj = s_ref[ks, :] + bias_ref[h, 1]
                    s_ref[ks, :] = sj
                    row = mask_ref[e, j:j + 1, qs] - far2
                else:
                    sj = s_ref[ks, :]
                    row = mask_ref[e, j:j + 1, qs]
                top = jnp.max(sj, axis=0, keepdims=True)
                rows.append(row)
                tops.append(top if row is None else top + row)
            m = functools.reduce(jnp.maximum, tops)
            for j in range(qi + 1):
                ks = slice(j * blk, (j + 1) * blk)
                shift = m if rows[j] is None else m - rows[j]
                p_ref[ks, :] = jnp.exp2(s_ref[ks, :] - shift).astype(BF16)
            ot = jnp.dot(vte_ref[e, :, 0:keys], p_ref[0:keys, :], preferred_element_type=F32)
            denom = ot[(1 - e) * HEAD_DIM:(1 - e) * HEAD_DIM + 1, :]
            ot_ref[e * HEAD_DIM:(e + 1) * HEAD_DIM, qs] = ot[e * HEAD_DIM:(e + 1) * HEAD_DIM, :] / denom
    o_ref[...] = ot_ref[...].T


def _moba_prompt(rel_bias, qt3, kt3, vt3):
    slab = pl.BlockSpec((None, PAIR, SEQ), lambda b, hp: (b, hp, 0))
    return pl.pallas_call(
        _moba_kernel,
        grid=(BATCH, N_HEADS // 2),
        in_specs=[pl.BlockSpec(memory_space=pltpu.SMEM), slab, slab, slab],
        out_specs=pl.BlockSpec((None, SEQ, PAIR), lambda b, hp: (b, 0, hp)),
        out_shape=jax.ShapeDtypeStruct((BATCH, SEQ, D_ATTN), F32),
        scratch_shapes=[
            pltpu.VMEM((SEQ, PAIR), BF16),
            pltpu.VMEM((2, PAIR, SEQ), BF16),
            pltpu.VMEM((2, N_PROMPT_BLOCKS, SEQ), F32),
            pltpu.VMEM((N_HEADS, 2, MOBA_BLOCK, MOBA_BLOCK), F32),
            pltpu.VMEM((SEQ, MOBA_BLOCK), F32),
            pltpu.VMEM((SEQ, MOBA_BLOCK), BF16),
            pltpu.VMEM((PAIR, SEQ), F32),
        ],
        compiler_params=pltpu.CompilerParams(dimension_semantics=("arbitrary", "arbitrary"),
                                             vmem_limit_bytes=VMEM_LIMIT),
        name="moba_prompt",
    )(rel_bias, qt3, kt3, vt3)


SCAN_PAGES = 16
SCAN_STEPS = N_PAGES // SCAN_PAGES
SCAN_BLOCKS = SCAN_PAGES // PAGES_PER_BLOCK


def _column(mat, n):
    lane = lax.broadcasted_iota(jnp.int32, mat.shape, 1)
    return jnp.sum(jnp.where(lane == n, mat, 0.0), axis=1, keepdims=True)


def _kscan_kernel(pt_ref, qt_ref, *refs):
    del pt_ref
    pages = refs[:SCAN_PAGES]
    lg_ref, idx_ref, qb_ref, g_ref = refs[SCAN_PAGES:]
    n = pl.program_id(0)
    c = pl.program_id(1)
    lane = lax.broadcasted_iota(jnp.int32, (N_HEADS, LANES), 1)
    sub = lax.broadcasted_iota(jnp.int32, (N_HEADS, LANES), 0)

    @pl.when(c == 0)
    def _():
        qb_ref[...] = jnp.broadcast_to(_column(qt_ref[...], n), (D_ATTN, LANES))
        g_ref[...] = jnp.full((N_HEADS, LANES), -jnp.inf, F32)

    g = g_ref[...]
    for bi in range(SCAN_BLOCKS):
        tile = jnp.zeros((N_HEADS, LANES), F32)
        for h in range(N_HEADS):
            qh = qb_ref[h * HEAD_DIM:(h + 1) * HEAD_DIM, :]
            block_row = None
            for half in range(PAGES_PER_BLOCK):
                pg = bi * PAGES_PER_BLOCK + half
                row = jnp.sum(pages[pg][h] * qh, axis=0, keepdims=True)
                lg_ref[h, pg:pg + 1, :] = row
                block_row = row if block_row is None else block_row + row
            tile = jnp.where(sub == h, block_row, tile)
        gcol = jnp.sum(tile, axis=1, keepdims=True) * (1.0 / MOBA_BLOCK)
        g = jnp.where(lane == c * SCAN_BLOCKS + bi, gcol, g)
    g_ref[...] = g

    @pl.when(c == SCAN_STEPS - 1)
    def _():
        lane_f = lane.astype(F32)
        gg = g
        out = jnp.zeros((N_HEADS, LANES), jnp.int32)
        for r in range(MOBA_TOPK):
            best = jnp.max(gg, axis=1, keepdims=True)
            pick = jnp.min(jnp.where(gg == best, lane_f, float(LANES)), axis=1, keepdims=True)
            out = jnp.where(lane == r, pick.astype(jnp.int32), out)
            gg = jnp.where(lane_f == pick, -jnp.inf, gg)
        idx_ref[...] = out


def _kscan_sample(pt_flat, qst, cache_kt):
    def page_spec(i):
        return pl.BlockSpec((None, N_HEADS, HEAD_DIM, PAGE_SIZE),
                            lambda n, c, pt: (pt[n * N_PAGES + c * SCAN_PAGES + i], 0, 0, 0))

    grid_spec = pltpu.PrefetchScalarGridSpec(
        num_scalar_prefetch=1,
        grid=(DEC_BATCH, SCAN_STEPS),
        in_specs=[pl.BlockSpec((D_ATTN, DEC_BATCH), lambda n, c, pt: (0, 0))]
        + [page_spec(i) for i in range(SCAN_PAGES)],
        out_specs=[
            pl.BlockSpec((None, N_HEADS, SCAN_PAGES, PAGE_SIZE), lambda n, c, pt: (n, 0, c, 0)),
            pl.BlockSpec((None, N_HEADS, LANES), lambda n, c, pt: (n, 0, 0)),
        ],
        scratch_shapes=[pltpu.VMEM((D_ATTN, LANES), F32), pltpu.VMEM((N_HEADS, LANES), F32)],
    )
    return pl.pallas_call(
        _kscan_kernel,
        grid_spec=grid_spec,
        out_shape=[jax.ShapeDtypeStruct((DEC_BATCH, N_HEADS, N_PAGES, PAGE_SIZE), F32),
                   jax.ShapeDtypeStruct((DEC_BATCH, N_HEADS, LANES), jnp.int32)],
        compiler_params=pltpu.CompilerParams(dimension_semantics=("arbitrary", "arbitrary"),
                                             vmem_limit_bytes=VMEM_LIMIT),
        name="kscan_sample",
    )(pt_flat, qst, *([cache_kt] * SCAN_PAGES))


SLABS = MOBA_TOPK * PAGES_PER_BLOCK
assert SLABS + 1 <= SUBLANES


def _sattn_copies(sel_ref, pt_ref, cv_hbm, vbuf, sem, n, slot):
    out = []
    for h in range(N_HEADS):
        for t in range(MOBA_TOPK):
            blk = sel_ref[(n * N_HEADS + h) * MOBA_TOPK + t]
            for half in range(PAGES_PER_BLOCK):
                page = pt_ref[n * N_PAGES + blk * PAGES_PER_BLOCK + half]
                out.append(pltpu.make_async_copy(cv_hbm.at[page, h], vbuf.at[slot, h, t * PAGES_PER_BLOCK + half],
                                                 sem.at[slot]))
    return out


def _sattn_kernel(sel_ref, pt_ref, rb_ref, lg_ref, qt_ref, kt_ref, vt_ref, cv_hbm, o_ref,
                  vbuf, tab_ref, self_ref, acc_ref, w_ref, sem):
    n = pl.program_id(0)
    slot = lax.rem(n, 2)
    copies = functools.partial(_sattn_copies, sel_ref, pt_ref, cv_hbm, vbuf, sem)
    scale = HEAD_DIM ** -0.5
    lane = lax.broadcasted_iota(jnp.int32, (SUBLANES, LANES), 1)
    sub = lax.broadcasted_iota(jnp.int32, (SUBLANES, LANES), 0)

    @pl.when(n == 0)
    def _():
        for c in copies(0, 0):
            c.start()
        dist = MOBA_BLOCK - lax.broadcasted_iota(jnp.int32, (1, MOBA_BLOCK), 1)
        prod = qt_ref[...] * kt_ref[...]
        for h in range(N_HEADS):
            tab_ref[h:h + 1, :] = _bias_of_distance(dist, rb_ref, h)
            self_ref[h:h + 1, :] = (jnp.sum(prod[h * HEAD_DIM:(h + 1) * HEAD_DIM, :], axis=0, keepdims=True) * scale
                                    + rb_ref[0, h])
        w_ref[...] = jnp.zeros((N_HEADS, LANES), F32)
        acc_ref[...] = jnp.zeros((D_ATTN, DEC_BATCH), F32)

    @pl.when(n + 1 < DEC_BATCH)
    def _():
        for c in copies(n + 1, 1 - slot):
            c.start()

    for c in copies(n, slot):
        c.wait()

    self_col = _column(self_ref[...], n)
    w_tile = w_ref[...]
    lane_wide = lax.broadcasted_iota(jnp.int32, (HEAD_DIM, LANES), 1)
    for h in range(N_HEADS):
        far_bias = rb_ref[N_BUCKETS - 1, h]
        s = jnp.full((SUBLANES, LANES), NEG, F32)
        for t in range(MOBA_TOPK):
            blk = sel_ref[(n * N_HEADS + h) * MOBA_TOPK + t]
            for half in range(PAGES_PER_BLOCK):
                r = t * PAGES_PER_BLOCK + half
                row = lg_ref[h, pl.ds(blk * PAGES_PER_BLOCK + half, 1), :] * scale
                bias = jnp.where(blk == N_PAST_BLOCKS - 1, tab_ref[h:h + 1, half * PAGE_SIZE:(half + 1) * PAGE_SIZE],
                                 far_bias)
                s = jnp.where(sub == r, row + bias, s)
        s_new = self_col[h:h + 1, :]
        m = jnp.maximum(jnp.max(jnp.max(s, axis=1, keepdims=True), axis=0, keepdims=True), s_new)
        p = jnp.exp(s - m)
        p_new = jnp.exp(s_new - m)
        l = jnp.sum(jnp.sum(p, axis=1, keepdims=True), axis=0, keepdims=True) + p_new
        acc = jnp.zeros((HEAD_DIM, LANES), F32)
        for r in range(SLABS):
            acc = acc + vbuf[slot, h, r] * p[r:r + 1, :]
        col = jnp.sum(acc, axis=1, keepdims=True) / l
        rows = slice(h * HEAD_DIM, (h + 1) * HEAD_DIM)
        acc_ref[rows, :] = jnp.where(lane_wide == n, col, acc_ref[rows, :])
        w_tile = jnp.where((sub == h) & (lane == n), p_new / l, w_tile)
    w_ref[...] = w_tile

    @pl.when(n == DEC_BATCH - 1)
    def _():
        for h in range(N_HEADS):
            rows = slice(h * HEAD_DIM, (h + 1) * HEAD_DIM)
            o_ref[rows, :] = acc_ref[rows, :] + w_tile[h:h + 1, :] * vt_ref[rows, :]


def _attn_sample(sel_flat, pt_flat, rel_bias, logits, qst, kst, vst, cache_vt):
    full = pl.BlockSpec((D_ATTN, DEC_BATCH), lambda n, sel, pt: (0, 0))
    grid_spec = pltpu.PrefetchScalarGridSpec(
        num_scalar_prefetch=2,
        grid=(DEC_BATCH,),
        in_specs=[
            pl.BlockSpec(memory_space=pltpu.SMEM),
            pl.BlockSpec((None, N_HEADS, N_PAGES, PAGE_SIZE), lambda n, sel, pt: (n, 0, 0, 0)),
            full, full, full,
            pl.BlockSpec(memory_space=pl.ANY),
        ],
        out_specs=full,
        scratch_shapes=[
            pltpu.VMEM((2, N_HEADS, SLABS, HEAD_DIM, PAGE_SIZE), F32),
            pltpu.VMEM((N_HEADS, MOBA_BLOCK), F32),
            pltpu.VMEM((N_HEADS, DEC_BATCH), F32),
            pltpu.VMEM((D_ATTN, DEC_BATCH), F32),
            pltpu.VMEM((N_HEADS, DEC_BATCH), F32),
            pltpu.SemaphoreType.DMA((2,)),
        ],
    )
    return pl.pallas_call(
        _sattn_kernel,
        grid_spec=grid_spec,
        out_shape=jax.ShapeDtypeStruct((D_ATTN, DEC_BATCH), F32),
        compiler_params=pltpu.CompilerParams(dimension_semantics=("arbitrary",), vmem_limit_bytes=VMEM_LIMIT),
        name="attn_sample",
    )(sel_flat, pt_flat, rel_bias, logits, qst, kst, vst, cache_vt)


FF_CHUNK = 1024


def _ffn_kernel(x_ref, attn_ref, conv_ref, gt1_ref, sh2_ref, sc2_ref, gt2_ref, ga_ref, gffn_ref, gfin_ref,
                wo_ref, w1_ref, w2_ref, y_ref):
    attn_n = _rms(attn_ref[...], ga_ref[...]).astype(BF16)
    mixed = (jnp.dot(attn_n, wo_ref[:D_ATTN, :], preferred_element_type=F32)
             + jnp.dot(conv_ref[...], wo_ref[D_ATTN:, :], preferred_element_type=F32))
    x1 = x_ref[...] + gt1_ref[...] * mixed
    h2 = (_rms(x1, gffn_ref[...]) * (1.0 + sc2_ref[...]) + sh2_ref[...]).astype(BF16)
    acc = jnp.zeros(x1.shape, F32)
    for c in range(D_FF // FF_CHUNK):
        f = jnp.dot(h2, w1_ref[:, c * FF_CHUNK:(c + 1) * FF_CHUNK], preferred_element_type=F32)
        f = jnp.square(jnp.maximum(f, 0.0)).astype(BF16)
        acc = acc + jnp.dot(f, w2_ref[c * FF_CHUNK:(c + 1) * FF_CHUNK, :], preferred_element_type=F32)
    x2 = x1 + gt2_ref[...] * acc
    y_ref[...] = _rms(x2, gfin_ref[...])


def _ffn(x2d, attn2d, conv2d, mod, mod_spec, g_attn, g_ffn, g_final, wo_b, w1_b, w2_b, tm):
    rows = x2d.shape[0]
    row_spec = lambda width: pl.BlockSpec((tm, width), lambda i: (i, 0))
    const = lambda shape: pl.BlockSpec(shape, lambda i: (0, 0), pipeline_mode=pl.Buffered(1))
    return pl.pallas_call(
        _ffn_kernel,
        grid=(rows // tm,),
        in_specs=[
            row_spec(D_MODEL), row_spec(D_ATTN), row_spec(D_CONV),
            mod_spec(2), mod_spec(3), mod_spec(4), mod_spec(5),
            const((1, D_ATTN)), const((1, D_MODEL)), const((1, D_MODEL)),
            const((D_MODEL, D_MODEL)), const((D_MODEL, D_FF)), const((D_FF, D_MODEL)),
        ],
        out_specs=row_spec(D_MODEL),
        out_shape=jax.ShapeDtypeStruct((rows, D_MODEL), F32),
        compiler_params=pltpu.CompilerParams(dimension_semantics=("arbitrary",), vmem_limit_bytes=VMEM_LIMIT),
        name="ffn",
    )(x2d, attn2d, conv2d, mod, mod, mod, mod, g_attn, g_ffn, g_final, wo_b, w1_b, w2_b)


def kernel(x_prompt, x_sample, c_prompt, c_sample, cache_k, cache_v, state_conv, page_table, rel_bias, w_ada, b_ada, g_mix, w_in, w_dw, b_dw, ln_conv_g, ln_conv_b, g_attn_out, g_conv_out, w_out, g_ffn, w_ff1, w_ff2, g_final):
    w_in_b = w_in[0].astype(BF16)
    wo_b = w_out[0].astype(BF16)
    w1_b = w_ff1[0].astype(BF16)
    w2_b = w_ff2[0].astype(BF16)
    g_fin = g_final.reshape(1, D_MODEL)

    mod = _mod(jnp.concatenate([c_prompt, c_sample], axis=0), w_ada[0], b_ada)
    mod_p = mod[:BATCH].reshape(BATCH, 6, 1, D_MODEL)
    mod_s = mod[BATCH:]

    tm = 512
    tiles_per_seq = SEQ // tm
    xp = x_prompt.reshape(BATCH * SEQ, D_MODEL)
    qt, kt, vt, u = _inproj(
        xp, mod_p, functools.partial(_mod_specs_prompt, tm), g_mix, w_in_b, tm, (BATCH, D_ATTN, SEQ),
        pl.BlockSpec((None, D_ATTN, tm), lambda i: (i // tiles_per_seq, 0, i % tiles_per_seq)))
    u3 = u.reshape(BATCH, SEQ, D_CONV)
    conv_n = _conv_prompt(u3, w_dw[0], b_dw, ln_conv_g, ln_conv_b, g_conv_out)
    attn = _moba_prompt(rel_bias, qt, kt, vt)
    y_p = _ffn(xp, attn.reshape(BATCH * SEQ, D_ATTN), conv_n.reshape(BATCH * SEQ, D_CONV), mod_p,
               functools.partial(_mod_specs_prompt, tm), g_attn_out, g_ffn, g_fin, wo_b, w1_b, w2_b, tm)

    xs = x_sample.reshape(DEC_BATCH, D_MODEL)
    qst, kst, vst, us = _inproj(xs, mod_s, _mod_specs_sample, g_mix, w_in_b, DEC_BATCH, (D_ATTN, DEC_BATCH),
                                pl.BlockSpec((D_ATTN, DEC_BATCH), lambda i: (0, 0)))
    hist_t = jnp.transpose(state_conv[0], (1, 0, 2))
    conv_s, new_hist_t = _conv_sample(hist_t, us, w_dw[0], b_dw, ln_conv_g, ln_conv_b, g_conv_out)
    cache_kt = jnp.transpose(cache_k[0], (0, 2, 3, 1))
    cache_vt = jnp.transpose(cache_v[0], (0, 2, 3, 1))
    pt_flat = page_table.reshape(-1)
    logits, sel = _kscan_sample(pt_flat, qst, cache_kt)
    sel_flat = sel[:, :, :MOBA_TOPK].reshape(-1)
    attn_st = _attn_sample(sel_flat, pt_flat, rel_bias, logits, qst, kst, vst, cache_vt)
    y_s = _ffn(xs, attn_st.T, conv_s, mod_s, _mod_specs_sample,
               g_attn_out, g_ffn, g_fin, wo_b, w1_b, w2_b, DEC_BATCH)

    kv_p = lambda a: jnp.transpose(a.reshape(1, BATCH, N_HEADS, HEAD_DIM, SEQ), (0, 1, 4, 2, 3))
    kv_s = lambda a: jnp.transpose(a.reshape(1, 1, N_HEADS, HEAD_DIM, DEC_BATCH), (0, 4, 1, 2, 3))
    hist_p = u3[:, SEQ - N_HIST:, :][None]
    hist_s = jnp.transpose(new_hist_t, (1, 0, 2))[None]
    return (y_p.reshape(BATCH, SEQ, D_MODEL), y_s.reshape(DEC_BATCH, 1, D_MODEL),
            kv_p(kt), kv_p(vt), hist_p, kv_s(kst), kv_s(vst), hist_s)
```

```python
import functools
import math

import numpy as np
import jax
import jax.numpy as jnp
from jax import lax
from jax.experimental import pallas as pl
from jax.experimental.pallas import tpu as pltpu

D_MODEL = 1024
BATCH = 8
SEQ = 2048
DEC_BATCH = 128
PAST_LEN = 8192
PAGE_SIZE = 128
D_ATTN = 512
D_CONV = 512
HEAD_DIM = 64
N_HEADS = 8
CONV_WIDTH = 31
MOBA_BLOCK = 256
MOBA_TOPK = 3
N_BUCKETS = 32
MAX_DISTANCE = 128
D_FF = 4096
EPS = 1e-6
D_IN = 3 * D_ATTN + 2 * D_CONV
N_PAGES = PAST_LEN // PAGE_SIZE
N_PAST_BLOCKS = PAST_LEN // MOBA_BLOCK
PAGES_PER_BLOCK = MOBA_BLOCK // PAGE_SIZE
N_PROMPT_BLOCKS = SEQ // MOBA_BLOCK
LANES = 128
SUBLANES = 8

F32 = jnp.float32
BF16 = jnp.bfloat16
NEG = -1e30
LOG2E = math.log2(math.e)
VMEM_LIMIT = 56 * 1024 * 1024


def _bucket_thresholds():
    n = np.arange(0, 4 * MAX_DISTANCE)
    max_exact = N_BUCKETS // 2
    ratio = np.maximum(n, max_exact).astype(np.float32) / np.float32(max_exact)
    val = np.log(ratio) / np.float32(math.log(MAX_DISTANCE / max_exact)) * np.float32(N_BUCKETS - max_exact)
    large = np.minimum(max_exact + val.astype(np.int32), N_BUCKETS - 1)
    bucket = np.where(n < max_exact, n, large)
    assert np.all(np.diff(bucket) >= 0) and bucket[-1] == N_BUCKETS - 1
    return [int(np.argmax(bucket >= b)) for b in range(N_BUCKETS)]


BUCKET_START = _bucket_thresholds()
FAR_DISTANCE = BUCKET_START[N_BUCKETS - 1]
assert FAR_DISTANCE <= MOBA_BLOCK


def _rms(x, g):
    return x * lax.rsqrt(jnp.mean(x * x, axis=-1, keepdims=True) + EPS) * g


def _bias_of_distance(dist, rb_ref, h):
    val = jnp.full(dist.shape, rb_ref[0, h], F32)
    for b in range(1, N_BUCKETS):
        val = jnp.where(dist >= BUCKET_START[b], rb_ref[b, h], val)
    return val


def _mod_kernel(c_ref, w_ref, b_ref, o_ref):
    c = c_ref[...]
    s = (c * jax.nn.sigmoid(c)).astype(BF16)
    o_ref[...] = jnp.dot(s, w_ref[...].astype(BF16), preferred_element_type=F32) + b_ref[...]


def _mod(c_all, w_ada, b_ada):
    rows = c_all.shape[0]
    bn = 1024
    return pl.pallas_call(
        _mod_kernel,
        grid=(6 * D_MODEL // bn,),
        in_specs=[
            pl.BlockSpec((rows, D_MODEL), lambda j: (0, 0)),
            pl.BlockSpec((D_MODEL, bn), lambda j: (0, j)),
            pl.BlockSpec((1, bn), lambda j: (0, j)),
        ],
        out_specs=pl.BlockSpec((rows, bn), lambda j: (0, j)),
        out_shape=jax.ShapeDtypeStruct((rows, 6 * D_MODEL), F32),
        compiler_params=pltpu.CompilerParams(dimension_semantics=("arbitrary",), vmem_limit_bytes=VMEM_LIMIT),
        name="mod",
    )(c_all, w_ada, b_ada)


def _mod_specs_prompt(tm, k):
    tiles_per_seq = SEQ // tm
    return pl.BlockSpec((None, None, 1, D_MODEL), lambda i, *_: (i // tiles_per_seq, k, 0, 0))


def _mod_specs_sample(k):
    return pl.BlockSpec((DEC_BATCH, D_MODEL), lambda i, *_: (0, k))


def _inproj_kernel(x_ref, sh_ref, sc_ref, g_ref, w_ref, qt_ref, kt_ref, vt_ref, u_ref):
    h = _rms(x_ref[...], g_ref[...]) * (1.0 + sc_ref[...]) + sh_ref[...]
    hb = h.astype(BF16)

    def proj(i):
        return jnp.dot(hb, w_ref[:, i * D_ATTN:(i + 1) * D_ATTN], preferred_element_type=F32)

    qt_ref[...] = proj(0).T
    kt_ref[...] = proj(1).T
    vt_ref[...] = proj(2).T
    a = proj(3)
    g = proj(4)
    u_ref[...] = a * jax.nn.sigmoid(g)


def _inproj(x2d, mod, mod_spec, g_mix, w_in_b, tm, t_shape, t_spec):
    rows = x2d.shape[0]
    row_spec = lambda width: pl.BlockSpec((tm, width), lambda i: (i, 0))
    t_out = jax.ShapeDtypeStruct(t_shape, F32)
    return pl.pallas_call(
        _inproj_kernel,
        grid=(rows // tm,),
        in_specs=[
            row_spec(D_MODEL),
            mod_spec(0),
            mod_spec(1),
            pl.BlockSpec((1, D_MODEL), lambda i: (0, 0)),
            pl.BlockSpec((D_MODEL, D_IN), lambda i: (0, 0)),
        ],
        out_specs=[t_spec, t_spec, t_spec, row_spec(D_CONV)],
        out_shape=[t_out, t_out, t_out, jax.ShapeDtypeStruct((rows, D_CONV), F32)],
        compiler_params=pltpu.CompilerParams(dimension_semantics=("arbitrary",), vmem_limit_bytes=VMEM_LIMIT),
        name="inproj",
    )(x2d, mod, mod, g_mix, w_in_b)


CONV_TM = 256
CONV_HALO = 32


def _conv_post(y, lng, lnb, gco):
    mu = jnp.mean(y, axis=-1, keepdims=True)
    yc = y - mu
    yn = yc * lax.rsqrt(jnp.mean(yc * yc, axis=-1, keepdims=True) + EPS) * lng + lnb
    s = yn * jax.nn.sigmoid(yn)
    return _rms(s, gco)


CONV_SPAN = CONV_HALO + CONV_TM - SUBLANES


def _conv_kernel(prev_ref, cur_ref, wdw_ref, bdw_ref, lng_ref, lnb_ref, gco_ref, o_ref, sh_ref):
    i = pl.program_id(1)
    prev = jnp.where(i == 0, 0.0, prev_ref[...])
    win = jnp.concatenate([prev, cur_ref[...]], axis=0)
    first = CONV_HALO - (CONV_WIDTH - 1)
    acc = jnp.zeros((CONV_TM, D_CONV), F32)
    for r in range(SUBLANES):
        offs = [o for o in range(first, first + CONV_WIDTH) if o % SUBLANES == r]
        if r > 0:
            sh_ref[r - 1] = win[r:r + CONV_SPAN, :]
        for o in offs:
            a8 = o - r
            tap = win[a8:a8 + CONV_TM, :] if r == 0 else sh_ref[r - 1, a8:a8 + CONV_TM, :]
            acc = acc + tap * wdw_ref[o - first:o - first + 1, :]
    y = acc + bdw_ref[...]
    o_ref[...] = _conv_post(y, lng_ref[...], lnb_ref[...], gco_ref[...]).astype(BF16)


def _conv_prompt(u3, w_dw, b_dw, ln_g, ln_b, g_co):
    halo_per_tile = CONV_TM // CONV_HALO
    vec = pl.BlockSpec((1, D_CONV), lambda b, i: (0, 0))
    return pl.pallas_call(
        _conv_kernel,
        grid=(BATCH, SEQ // CONV_TM),
        in_specs=[
            pl.BlockSpec((None, CONV_HALO, D_CONV), lambda b, i: (b, jnp.maximum(i * halo_per_tile - 1, 0), 0)),
            pl.BlockSpec((None, CONV_TM, D_CONV), lambda b, i: (b, i, 0)),
            pl.BlockSpec((CONV_WIDTH, D_CONV), lambda b, i: (0, 0)),
            vec, vec, vec, vec,
        ],
        out_specs=pl.BlockSpec((None, CONV_TM, D_CONV), lambda b, i: (b, i, 0)),
        out_shape=jax.ShapeDtypeStruct((BATCH, SEQ, D_CONV), BF16),
        scratch_shapes=[pltpu.VMEM((SUBLANES - 1, CONV_SPAN, D_CONV), F32)],
        compiler_params=pltpu.CompilerParams(dimension_semantics=("arbitrary", "arbitrary"),
                                             vmem_limit_bytes=VMEM_LIMIT),
        name="conv_prompt",
    )(u3, u3, w_dw, b_dw, ln_g, ln_b, g_co)


SCONV_TN = 32
N_HIST = CONV_WIDTH - 1


def _sconv_kernel(hist_ref, u_ref, wdw_ref, bdw_ref, lng_ref, lnb_ref, gco_ref, o_ref, nh_ref):
    u = u_ref[...]
    y = u * wdw_ref[N_HIST:CONV_WIDTH, :] + bdw_ref[...]
    for w in range(N_HIST):
        y = y + hist_ref[w] * wdw_ref[w:w + 1, :]
    o_ref[...] = _conv_post(y, lng_ref[...], lnb_ref[...], gco_ref[...]).astype(BF16)
    for w in range(N_HIST - 1):
        nh_ref[w] = hist_ref[w + 1]
    nh_ref[N_HIST - 1] = u


def _conv_sample(hist_t, u, w_dw, b_dw, ln_g, ln_b, g_co):
    vec = pl.BlockSpec((1, D_CONV), lambda i: (0, 0))
    hist_spec = pl.BlockSpec((N_HIST, SCONV_TN, D_CONV), lambda i: (0, i, 0))
    return pl.pallas_call(
        _sconv_kernel,
        grid=(DEC_BATCH // SCONV_TN,),
        in_specs=[
            hist_spec,
            pl.BlockSpec((SCONV_TN, D_CONV), lambda i: (i, 0)),
            pl.BlockSpec((CONV_WIDTH, D_CONV), lambda i: (0, 0)),
            vec, vec, vec, vec,
        ],
        out_specs=[pl.BlockSpec((SCONV_TN, D_CONV), lambda i: (i, 0)), hist_spec],
        out_shape=[jax.ShapeDtypeStruct((DEC_BATCH, D_CONV), BF16),
                   jax.ShapeDtypeStruct((N_HIST, DEC_BATCH, D_CONV), F32)],
        compiler_params=pltpu.CompilerParams(dimension_semantics=("arbitrary",), vmem_limit_bytes=VMEM_LIMIT),
        name="conv_sample",
    )(hist_t, u, w_dw, b_dw, ln_g, ln_b, g_co)


PAIR = 2 * HEAD_DIM


def _moba_kernel(rb_ref, qt_ref, kt_ref, vt_ref, o_ref, kb_ref, vte_ref, mask_ref, bias_ref, s_ref, p_ref, ot_ref):
    b = pl.program_id(0)
    hp = pl.program_id(1)
    blk = MOBA_BLOCK
    nb = N_PROMPT_BLOCKS
    drow = lax.broadcasted_iota(jnp.int32, (PAIR, 1), 0)
    head_rows = [(drow >= e * HEAD_DIM) & (drow < (e + 1) * HEAD_DIM) for e in range(2)]

    @pl.when(b == 0)
    def _():
        kk = lax.broadcasted_iota(jnp.int32, (blk, blk), 0)
        qq = lax.broadcasted_iota(jnp.int32, (blk, blk), 1)
        d0 = qq - kk
        for e in range(2):
            h = 2 * hp + e
            bias_ref[h, 0] = jnp.where(d0 >= 0, _bias_of_distance(jnp.maximum(d0, 0), rb_ref, h) * LOG2E, NEG)
            bias_ref[h, 1] = _bias_of_distance(d0 + blk, rb_ref, h) * LOG2E

    km_cols = []
    for j in range(nb):
        ktj = kt_ref[:, j * blk:(j + 1) * blk]
        kb_ref[j * blk:(j + 1) * blk, :] = ktj.T.astype(BF16)
        km_cols.append(jnp.sum(ktj, axis=1, keepdims=True) * (1.0 / blk))
    vt = vt_ref[...]
    for e in range(2):
        vte_ref[e] = jnp.where(head_rows[e], vt, 1.0).astype(BF16)

    qt = qt_ref[...]
    qblk = lax.broadcasted_iota(jnp.int32, (1, SEQ), 1) // blk
    for e in range(2):
        h = 2 * hp + e
        far2 = rb_ref[N_BUCKETS - 1, h] * LOG2E
        qh = qt[e * HEAD_DIM:(e + 1) * HEAD_DIM, :]
        gates = [jnp.sum(qh * km_cols[j][e * HEAD_DIM:(e + 1) * HEAD_DIM, :], axis=0, keepdims=True)
                 for j in range(nb - 1)]
        for j in range(nb - 1):
            cnt = jnp.zeros((1, SEQ), jnp.int32)
            for i in range(nb - 1):
                if i == j:
                    continue
                beats = (gates[i] > gates[j]) if i > j else (gates[i] >= gates[j])
                cnt = cnt + jnp.where(beats, jnp.where(i < qblk, 1, 0), 0)
            keep = jnp.where(j < qblk, jnp.where(cnt < MOBA_TOPK, 0.0, NEG), NEG)
            mask_ref[e, j:j + 1, :] = keep + far2

    qscale = (HEAD_DIM ** -0.5) * LOG2E
    for qi in range(nb):
        qs = slice(qi * blk, (qi + 1) * blk)
        keys = (qi + 1) * blk
        for e in range(2):
            h = 2 * hp + e
            far2 = rb_ref[N_BUCKETS - 1, h] * LOG2E
            qtm = jnp.where(head_rows[e], qt_ref[:, qs] * qscale, 0.0).astype(BF16)
            s_ref[0:keys, :] = jnp.dot(kb_ref[0:keys, :], qtm, preferred_element_type=F32)
            rows = []
            tops = []
            for j in range(qi + 1):
                ks = slice(j * blk, (j + 1) * blk)
                if j == qi:
                    sj = s_ref[ks, :] + bias_ref[h, 0]
                    s_ref[ks, :] = sj
                    row = None
                elif j == qi - 1:
                    sj = s_ref[ks, :] + bias_ref[h, 1]
                    s_ref[ks, :] = sj
                    row = mask_ref[e, j:j + 1, qs] - far2
                else:
                    sj = s_ref[ks, :]
                    row = mask_ref[e, j:j + 1, qs]
                top = jnp.max(sj, axis=0, keepdims=True)
                rows.append(row)
                tops.append(top if row is None else top + row)
            m = functools.reduce(jnp.maximum, tops)
            for j in range(qi + 1):
                ks = slice(j * blk, (j + 1) * blk)
                shift = m if rows[j] is None else m - rows[j]
                p_ref[ks, :] = jnp.exp2(s_ref[ks, :] - shift).astype(BF16)
            ot = jnp.dot(vte_ref[e, :, 0:keys], p_ref[0:keys, :], preferred_element_type=F32)
            denom = ot[(1 - e) * HEAD_DIM:(1 - e) * HEAD_DIM + 1, :]
            ot_ref[e * HEAD_DIM:(e + 1) * HEAD_DIM, qs] = ot[e * HEAD_DIM:(e + 1) * HEAD_DIM, :] / denom
    o_ref[...] = ot_ref[...].T


def _moba_prompt(rel_bias, qt3, kt3, vt3):
    slab = pl.BlockSpec((None, PAIR, SEQ), lambda b, hp: (b, hp, 0))
    return pl.pallas_call(
        _moba_kernel,
        grid=(BATCH, N_HEADS // 2),
        in_specs=[pl.BlockSpec(memory_space=pltpu.SMEM), slab, slab, slab],
        out_specs=pl.BlockSpec((None, SEQ, PAIR), lambda b, hp: (b, 0, hp)),
        out_shape=jax.ShapeDtypeStruct((BATCH, SEQ, D_ATTN), F32),
        scratch_shapes=[
            pltpu.VMEM((SEQ, PAIR), BF16),
            pltpu.VMEM((2, PAIR, SEQ), BF16),
            pltpu.VMEM((2, N_PROMPT_BLOCKS, SEQ), F32),
            pltpu.VMEM((N_HEADS, 2, MOBA_BLOCK, MOBA_BLOCK), F32),
            pltpu.VMEM((SEQ, MOBA_BLOCK), F32),
            pltpu.VMEM((SEQ, MOBA_BLOCK), BF16),
            pltpu.VMEM((PAIR, SEQ), F32),
        ],
        compiler_params=pltpu.CompilerParams(dimension_semantics=("arbitrary", "arbitrary"),
                                             vmem_limit_bytes=VMEM_LIMIT),
        name="moba_prompt",
    )(rel_bias, qt3, kt3, vt3)


def _column(mat, n):
    lane = lax.broadcasted_iota(jnp.int32, mat.shape, 1)
    return jnp.sum(jnp.where(lane == n, mat, 0.0), axis=1, keepdims=True)


def _scan_pages(page, n_pages, qb_ref, store_row, g, first_block):
    lane = lax.broadcasted_iota(jnp.int32, (N_HEADS, LANES), 1)
    sub = lax.broadcasted_iota(jnp.int32, (N_HEADS, LANES), 0)
    for bi in range(n_pages // PAGES_PER_BLOCK):
        tile = jnp.zeros((N_HEADS, LANES), F32)
        for h in range(N_HEADS):
            qh = qb_ref[h * HEAD_DIM:(h + 1) * HEAD_DIM, :]
            block_row = None
            for half in range(PAGES_PER_BLOCK):
                i = bi * PAGES_PER_BLOCK + half
                row = jnp.sum(page(i)[h] * qh, axis=0, keepdims=True)
                store_row(h, i, row)
                block_row = row if block_row is None else block_row + row
            tile = jnp.where(sub == h, block_row, tile)
        gcol = jnp.sum(tile, axis=1, keepdims=True) * (1.0 / MOBA_BLOCK)
        g = jnp.where(lane == first_block + bi, gcol, g)
    return g


def _top_blocks(g):
    lane = lax.broadcasted_iota(jnp.int32, (N_HEADS, LANES), 1)
    lane_f = lane.astype(F32)
    out = jnp.zeros((N_HEADS, LANES), jnp.int32)
    for r in range(MOBA_TOPK):
        best = jnp.max(g, axis=1, keepdims=True)
        pick = jnp.min(jnp.where(g == best, lane_f, float(LANES)), axis=1, keepdims=True)
        out = jnp.where(lane == r, pick.astype(jnp.int32), out)
        g = jnp.where(lane_f == pick, -jnp.inf, g)
    return out


SLABS = MOBA_TOPK * PAGES_PER_BLOCK
assert SLABS + 1 <= SUBLANES


def _sattn_copies(sel_ref, pt_ref, cv_hbm, vbuf, sem, n, slot):
    out = []
    for h in range(N_HEADS):
        for t in range(MOBA_TOPK):
            blk = sel_ref[(n * N_HEADS + h) * MOBA_TOPK + t]
            for half in range(PAGES_PER_BLOCK):
                page = pt_ref[n * N_PAGES + blk * PAGES_PER_BLOCK + half]
                out.append(pltpu.make_async_copy(cv_hbm.at[page, h], vbuf.at[slot, h, t * PAGES_PER_BLOCK + half],
                                                 sem.at[slot]))
    return out


def _sattn_kernel(sel_ref, pt_ref, rb_ref, lg_ref, qt_ref, kt_ref, vt_ref, cv_hbm, o_ref,
                  vbuf, tab_ref, self_ref, acc_ref, w_ref, sem):
    n = pl.program_id(0)
    slot = lax.rem(n, 2)
    copies = functools.partial(_sattn_copies, sel_ref, pt_ref, cv_hbm, vbuf, sem)
    scale = HEAD_DIM ** -0.5
    lane = lax.broadcasted_iota(jnp.int32, (SUBLANES, LANES), 1)
    sub = lax.broadcasted_iota(jnp.int32, (SUBLANES, LANES), 0)

    @pl.when(n == 0)
    def _():
        for c in copies(0, 0):
            c.start()
        dist = MOBA_BLOCK - lax.broadcasted_iota(jnp.int32, (1, MOBA_BLOCK), 1)
        prod = qt_ref[...] * kt_ref[...]
        for h in range(N_HEADS):
            tab_ref[h:h + 1, :] = _bias_of_distance(dist, rb_ref, h)
            self_ref[h:h + 1, :] = (jnp.sum(prod[h * HEAD_DIM:(h + 1) * HEAD_DIM, :], axis=0, keepdims=True) * scale
                                    + rb_ref[0, h])
        w_ref[...] = jnp.zeros((N_HEADS, LANES), F32)
        acc_ref[...] = jnp.zeros((D_ATTN, DEC_BATCH), F32)

    @pl.when(n + 1 < DEC_BATCH)
    def _():
        for c in copies(n + 1, 1 - slot):
            c.start()

    for c in copies(n, slot):
        c.wait()

    self_col = _column(self_ref[...], n)
    w_tile = w_ref[...]
    lane_wide = lax.broadcasted_iota(jnp.int32, (HEAD_DIM, LANES), 1)
    for h in range(N_HEADS):
        far_bias = rb_ref[N_BUCKETS - 1, h]
        s = jnp.full((SUBLANES, LANES), NEG, F32)
        for t in range(MOBA_TOPK):
            blk = sel_ref[(n * N_HEADS + h) * MOBA_TOPK + t]
            for half in range(PAGES_PER_BLOCK):
                r = t * PAGES_PER_BLOCK + half
                row = lg_ref[h, pl.ds(blk * PAGES_PER_BLOCK + half, 1), :] * scale
                bias = jnp.where(blk == N_PAST_BLOCKS - 1, tab_ref[h:h + 1, half * PAGE_SIZE:(half + 1) * PAGE_SIZE],
                                 far_bias)
                s = jnp.where(sub == r, row + bias, s)
        s_new = self_col[h:h + 1, :]
        m = jnp.maximum(jnp.max(jnp.max(s, axis=1, keepdims=True), axis=0, keepdims=True), s_new)
        p = jnp.exp(s - m)
        p_new = jnp.exp(s_new - m)
        l = jnp.sum(jnp.sum(p, axis=1, keepdims=True), axis=0, keepdims=True) + p_new
        acc = jnp.zeros((HEAD_DIM, LANES), F32)
        for r in range(SLABS):
            acc = acc + vbuf[slot, h, r] * p[r:r + 1, :]
        col = jnp.sum(acc, axis=1, keepdims=True) / l
        rows = slice(h * HEAD_DIM, (h + 1) * HEAD_DIM)
        acc_ref[rows, :] = jnp.where(lane_wide == n, col, acc_ref[rows, :])
        w_tile = jnp.where((sub == h) & (lane == n), p_new / l, w_tile)
    w_ref[...] = w_tile

    @pl.when(n == DEC_BATCH - 1)
    def _():
        for h in range(N_HEADS):
            rows = slice(h * HEAD_DIM, (h + 1) * HEAD_DIM)
            o_ref[rows, :] = acc_ref[rows, :] + w_tile[h:h + 1, :] * vt_ref[rows, :]


def _attn_sample(sel_flat, pt_flat, rel_bias, logits, qst, kst, vst, cache_vt):
    full = pl.BlockSpec((D_ATTN, DEC_BATCH), lambda n, sel, pt: (0, 0))
    grid_spec = pltpu.PrefetchScalarGridSpec(
        num_scalar_prefetch=2,
        grid=(DEC_BATCH,),
        in_specs=[
            pl.BlockSpec(memory_space=pltpu.SMEM),
            pl.BlockSpec((None, N_HEADS, N_PAGES, PAGE_SIZE), lambda n, sel, pt: (n, 0, 0, 0)),
            full, full, full,
            pl.BlockSpec(memory_space=pl.ANY),
        ],
        out_specs=full,
        scratch_shapes=[
            pltpu.VMEM((2, N_HEADS, SLABS, HEAD_DIM, PAGE_SIZE), F32),
            pltpu.VMEM((N_HEADS, MOBA_BLOCK), F32),
            pltpu.VMEM((N_HEADS, DEC_BATCH), F32),
            pltpu.VMEM((D_ATTN, DEC_BATCH), F32),
            pltpu.VMEM((N_HEADS, DEC_BATCH), F32),
            pltpu.SemaphoreType.DMA((2,)),
        ],
    )
    return pl.pallas_call(
        _sattn_kernel,
        grid_spec=grid_spec,
        out_shape=jax.ShapeDtypeStruct((D_ATTN, DEC_BATCH), F32),
        compiler_params=pltpu.CompilerParams(dimension_semantics=("arbitrary",), vmem_limit_bytes=VMEM_LIMIT),
        name="attn_sample",
    )(sel_flat, pt_flat, rel_bias, logits, qst, kst, vst, cache_vt)


FF_CHUNK = 1024
N_FF_CHUNKS = D_FF // FF_CHUNK


def _ffn_head(x_ref, attn_ref, conv_ref, gt1_ref, sh2_ref, sc2_ref, ga_ref, gffn_ref, wo_ref):
    attn_n = _rms(attn_ref[...], ga_ref[...]).astype(BF16)
    mixed = (jnp.dot(attn_n, wo_ref[:D_ATTN, :], preferred_element_type=F32)
             + jnp.dot(conv_ref[...], wo_ref[D_ATTN:, :], preferred_element_type=F32))
    x1 = x_ref[...] + gt1_ref[...] * mixed
    h2 = (_rms(x1, gffn_ref[...]) * (1.0 + sc2_ref[...]) + sh2_ref[...]).astype(BF16)
    return x1, h2


def _ffn_chunk(h2, w1_ref, w2_ref, c):
    f = jnp.dot(h2, w1_ref[:, c * FF_CHUNK:(c + 1) * FF_CHUNK], preferred_element_type=F32)
    f = jnp.square(jnp.maximum(f, 0.0)).astype(BF16)
    return jnp.dot(f, w2_ref[c * FF_CHUNK:(c + 1) * FF_CHUNK, :], preferred_element_type=F32)


def _ffn_kernel(x_ref, attn_ref, conv_ref, gt1_ref, sh2_ref, sc2_ref, gt2_ref, ga_ref, gffn_ref, gfin_ref,
                wo_ref, w1_ref, w2_ref, y_ref):
    x1, h2 = _ffn_head(x_ref, attn_ref, conv_ref, gt1_ref, sh2_ref, sc2_ref, ga_ref, gffn_ref, wo_ref)
    acc = _ffn_chunk(h2, w1_ref, w2_ref, 0)
    for c in range(1, N_FF_CHUNKS):
        acc = acc + _ffn_chunk(h2, w1_ref, w2_ref, c)
    y_ref[...] = _rms(x1 + gt2_ref[...] * acc, gfin_ref[...])


def _ffn_in_specs(mod_spec, tm):
    row_spec = lambda width: pl.BlockSpec((tm, width), lambda i, *_: (i, 0))
    const = lambda shape: pl.BlockSpec(shape, lambda i, *_: (0, 0), pipeline_mode=pl.Buffered(1))
    return [
        row_spec(D_MODEL), row_spec(D_ATTN), row_spec(D_CONV),
        mod_spec(2), mod_spec(3), mod_spec(4), mod_spec(5),
        const((1, D_ATTN)), const((1, D_MODEL)), const((1, D_MODEL)),
        const((D_MODEL, D_MODEL)), const((D_MODEL, D_FF)), const((D_FF, D_MODEL)),
    ]


def _ffn(x2d, attn2d, conv2d, mod, mod_spec, g_attn, g_ffn, g_final, wo_b, w1_b, w2_b, tm):
    rows = x2d.shape[0]
    return pl.pallas_call(
        _ffn_kernel,
        grid=(rows // tm,),
        in_specs=_ffn_in_specs(mod_spec, tm),
        out_specs=pl.BlockSpec((tm, D_MODEL), lambda i: (i, 0)),
        out_shape=jax.ShapeDtypeStruct((rows, D_MODEL), F32),
        compiler_params=pltpu.CompilerParams(dimension_semantics=("arbitrary",), vmem_limit_bytes=VMEM_LIMIT),
        name="ffn",
    )(x2d, attn2d, conv2d, mod, mod, mod, mod, g_attn, g_ffn, g_final, wo_b, w1_b, w2_b)


SCAN_TM = 256
SCAN_REGION_PAGES = 32
REGIONS_PER_SEQ = N_PAGES // SCAN_REGION_PAGES
SCAN_SEQS = N_FF_CHUNKS // REGIONS_PER_SEQ
SCAN_GRID = BATCH * SEQ // SCAN_TM
REGION_BLOCKS = SCAN_REGION_PAGES // PAGES_PER_BLOCK
assert SCAN_GRID * SCAN_SEQS == DEC_BATCH and N_FF_CHUNKS % 2 == 0 and N_FF_CHUNKS % REGIONS_PER_SEQ == 0


def _region_copies(pt_ref, ck_hbm, kbuf, sem, region, slot):
    return [pltpu.make_async_copy(ck_hbm.at[pt_ref[region * SCAN_REGION_PAGES + i]], kbuf.at[slot, i], sem.at[slot])
            for i in range(SCAN_REGION_PAGES)]


def _ffn_scan_kernel(pt_ref, x_ref, attn_ref, conv_ref, gt1_ref, sh2_ref, sc2_ref, gt2_ref, ga_ref, gffn_ref,
                     gfin_ref, wo_ref, w1_ref, w2_ref, qst_ref, ck_hbm, y_ref, lg_ref, idx_ref, kbuf, qb_ref, sem):
    step = pl.program_id(0)
    last_region = SCAN_GRID * N_FF_CHUNKS - 1
    copies = functools.partial(_region_copies, pt_ref, ck_hbm, kbuf, sem)

    @pl.when(step == 0)
    def _():
        for c in copies(0, 0):
            c.start()

    x1 = h2 = acc = g = None
    for r in range(N_FF_CHUNKS):
        region = step * N_FF_CHUNKS + r
        slot = r % 2
        for c in copies(jnp.minimum(region + 1, last_region), 1 - slot):
            c.start()
        for c in copies(region, slot):
            c.wait()

        seq_local, part = divmod(r, REGIONS_PER_SEQ)
        if part == 0:
            n = step * SCAN_SEQS + seq_local
            qb_ref[...] = jnp.broadcast_to(_column(qst_ref[...], n), (D_ATTN, LANES))
            g = jnp.full((N_HEADS, LANES), -jnp.inf, F32)

        def store_row(h, i, row, seq_local=seq_local, part=part):
            pg = part * SCAN_REGION_PAGES + i
            lg_ref[seq_local, h, pg:pg + 1, :] = row

        g = _scan_pages(lambda i, slot=slot: kbuf.at[slot, i], SCAN_REGION_PAGES, qb_ref, store_row, g,
                        part * REGION_BLOCKS)
        if part == REGIONS_PER_SEQ - 1:
            idx_ref[seq_local] = _top_blocks(g)

        if r == 0:
            x1, h2 = _ffn_head(x_ref, attn_ref, conv_ref, gt1_ref, sh2_ref, sc2_ref, ga_ref, gffn_ref, wo_ref)
            acc = _ffn_chunk(h2, w1_ref, w2_ref, 0)
        else:
            acc = acc + _ffn_chunk(h2, w1_ref, w2_ref, r)
    y_ref[...] = _rms(x1 + gt2_ref[...] * acc, gfin_ref[...])

    @pl.when(step == SCAN_GRID - 1)
    def _():
        for c in copies(last_region, 0):
            c.wait()


def _ffn_scan(pt_flat, x2d, attn2d, conv2d, mod, mod_spec, g_attn, g_ffn, g_final, wo_b, w1_b, w2_b, qst, cache_kt):
    grid_spec = pltpu.PrefetchScalarGridSpec(
        num_scalar_prefetch=1,
        grid=(SCAN_GRID,),
        in_specs=_ffn_in_specs(mod_spec, SCAN_TM) + [
            pl.BlockSpec((D_ATTN, DEC_BATCH), lambda i, pt: (0, 0)),
            pl.BlockSpec(memory_space=pl.ANY),
        ],
        out_specs=[
            pl.BlockSpec((SCAN_TM, D_MODEL), lambda i, pt: (i, 0)),
            pl.BlockSpec((SCAN_SEQS, N_HEADS, N_PAGES, PAGE_SIZE), lambda i, pt: (i, 0, 0, 0)),
            pl.BlockSpec((SCAN_SEQS, N_HEADS, LANES), lambda i, pt: (i, 0, 0)),
        ],
        scratch_shapes=[
            pltpu.VMEM((2, SCAN_REGION_PAGES, N_HEADS, HEAD_DIM, PAGE_SIZE), F32),
            pltpu.VMEM((D_ATTN, LANES), F32),
            pltpu.SemaphoreType.DMA((2,)),
        ],
    )
    return pl.pallas_call(
        _ffn_scan_kernel,
        grid_spec=grid_spec,
        out_shape=[jax.ShapeDtypeStruct((BATCH * SEQ, D_MODEL), F32),
                   jax.ShapeDtypeStruct((DEC_BATCH, N_HEADS, N_PAGES, PAGE_SIZE), F32),
                   jax.ShapeDtypeStruct((DEC_BATCH, N_HEADS, LANES), jnp.int32)],
        compiler_params=pltpu.CompilerParams(dimension_semantics=("arbitrary",), vmem_limit_bytes=VMEM_LIMIT),
        name="ffn_scan",
    )(pt_flat, x2d, attn2d, conv2d, mod, mod, mod, mod, g_attn, g_ffn, g_final, wo_b, w1_b, w2_b, qst, cache_kt)


def kernel(x_prompt, x_sample, c_prompt, c_sample, cache_k, cache_v, state_conv, page_table, rel_bias, w_ada, b_ada, g_mix, w_in, w_dw, b_dw, ln_conv_g, ln_conv_b, g_attn_out, g_conv_out, w_out, g_ffn, w_ff1, w_ff2, g_final):
    w_in_b = w_in[0].astype(BF16)
    wo_b = w_out[0].astype(BF16)
    w1_b = w_ff1[0].astype(BF16)
    w2_b = w_ff2[0].astype(BF16)
    g_fin = g_final.reshape(1, D_MODEL)

    mod = _mod(jnp.concatenate([c_prompt, c_sample], axis=0), w_ada[0], b_ada)
    mod_p = mod[:BATCH].reshape(BATCH, 6, 1, D_MODEL)
    mod_s = mod[BATCH:]

    tm = 512
    tiles_per_seq = SEQ // tm
    xp = x_prompt.reshape(BATCH * SEQ, D_MODEL)
    qt, kt, vt, u = _inproj(
        xp, mod_p, functools.partial(_mod_specs_prompt, tm), g_mix, w_in_b, tm, (BATCH, D_ATTN, SEQ),
        pl.BlockSpec((None, D_ATTN, tm), lambda i: (i // tiles_per_seq, 0, i % tiles_per_seq)))
    u3 = u.reshape(BATCH, SEQ, D_CONV)
    conv_n = _conv_prompt(u3, w_dw[0], b_dw, ln_conv_g, ln_conv_b, g_conv_out)
    attn = _moba_prompt(rel_bias, qt, kt, vt)

    xs = x_sample.reshape(DEC_BATCH, D_MODEL)
    qst, kst, vst, us = _inproj(xs, mod_s, _mod_specs_sample, g_mix, w_in_b, DEC_BATCH, (D_ATTN, DEC_BATCH),
                                pl.BlockSpec((D_ATTN, DEC_BATCH), lambda i: (0, 0)))
    hist_t = jnp.transpose(state_conv[0], (1, 0, 2))
    conv_s, new_hist_t = _conv_sample(hist_t, us, w_dw[0], b_dw, ln_conv_g, ln_conv_b, g_conv_out)
    cache_kt = jnp.transpose(cache_k[0], (0, 2, 3, 1))
    cache_vt = jnp.transpose(cache_v[0], (0, 2, 3, 1))
    pt_flat = page_table.reshape(-1)
    y_p, logits, sel = _ffn_scan(
        pt_flat, xp, attn.reshape(BATCH * SEQ, D_ATTN), conv_n.reshape(BATCH * SEQ, D_CONV), mod_p,
        functools.partial(_mod_specs_prompt, SCAN_TM), g_attn_out, g_ffn, g_fin, wo_b, w1_b, w2_b, qst, cache_kt)
    sel_flat = sel[:, :, :MOBA_TOPK].reshape(-1)
    attn_st = _attn_sample(sel_flat, pt_flat, rel_bias, logits, qst, kst, vst, cache_vt)
    y_s = _ffn(xs, attn_st.T, conv_s, mod_s, _mod_specs_sample,
               g_attn_out, g_ffn, g_fin, wo_b, w1_b, w2_b, DEC_BATCH)

    kv_p = lambda a: jnp.transpose(a.reshape(1, BATCH, N_HEADS, HEAD_DIM, SEQ), (0, 1, 4, 2, 3))
    kv_s = lambda a: jnp.transpose(a.reshape(1, 1, N_HEADS, HEAD_DIM, DEC_BATCH), (0, 4, 1, 2, 3))
    hist_p = u3[:, SEQ - N_HIST:, :][None]
    hist_s = jnp.transpose(new_hist_t, (1, 0, 2))[None]
    return (y_p.reshape(BATCH, SEQ, D_MODEL), y_s.reshape(DEC_BATCH, 1, D_MODEL),
            kv_p(kt), kv_p(vt), hist_p, kv_s(kst), kv_s(vst), hist_s)
```

```python
import functools
import math

import numpy as np
import jax
import jax.numpy as jnp
from jax import lax
from jax.experimental import pallas as pl
from jax.experimental.pallas import tpu as pltpu

D_MODEL = 1024
BATCH = 8
SEQ = 2048
DEC_BATCH = 128
PAST_LEN = 8192
PAGE_SIZE = 128
D_ATTN = 512
D_CONV = 512
HEAD_DIM = 64
N_HEADS = 8
CONV_WIDTH = 31
MOBA_BLOCK = 256
MOBA_TOPK = 3
N_BUCKETS = 32
MAX_DISTANCE = 128
D_FF = 4096
EPS = 1e-6
D_IN = 3 * D_ATTN + 2 * D_CONV
N_PAGES = PAST_LEN // PAGE_SIZE
N_PAST_BLOCKS = PAST_LEN // MOBA_BLOCK
PAGES_PER_BLOCK = MOBA_BLOCK // PAGE_SIZE
N_PROMPT_BLOCKS = SEQ // MOBA_BLOCK
LANES = 128
SUBLANES = 8
N_DMA_THREADS = 2

F32 = jnp.float32
BF16 = jnp.bfloat16
NEG = -1e30
LOG2E = math.log2(math.e)
VMEM_LIMIT = 56 * 1024 * 1024


def _bucket_thresholds():
    n = np.arange(0, 4 * MAX_DISTANCE)
    max_exact = N_BUCKETS // 2
    ratio = np.maximum(n, max_exact).astype(np.float32) / np.float32(max_exact)
    val = np.log(ratio) / np.float32(math.log(MAX_DISTANCE / max_exact)) * np.float32(N_BUCKETS - max_exact)
    large = np.minimum(max_exact + val.astype(np.int32), N_BUCKETS - 1)
    bucket = np.where(n < max_exact, n, large)
    assert np.all(np.diff(bucket) >= 0) and bucket[-1] == N_BUCKETS - 1
    return [int(np.argmax(bucket >= b)) for b in range(N_BUCKETS)]


BUCKET_START = _bucket_thresholds()
FAR_DISTANCE = BUCKET_START[N_BUCKETS - 1]
assert FAR_DISTANCE <= MOBA_BLOCK


def _rms(x, g):
    return x * lax.rsqrt(jnp.mean(x * x, axis=-1, keepdims=True) + EPS) * g


def _bias_of_distance(dist, rb_ref, h):
    val = jnp.full(dist.shape, rb_ref[0, h], F32)
    for b in range(1, N_BUCKETS):
        val = jnp.where(dist >= BUCKET_START[b], rb_ref[b, h], val)
    return val


def _mod_kernel(c_ref, w_ref, b_ref, o_ref):
    c = c_ref[...]
    s = (c * jax.nn.sigmoid(c)).astype(BF16)
    o_ref[...] = jnp.dot(s, w_ref[...].astype(BF16), preferred_element_type=F32) + b_ref[...]


def _mod(c_all, w_ada, b_ada):
    rows = c_all.shape[0]
    bn = 1024
    return pl.pallas_call(
        _mod_kernel,
        grid=(6 * D_MODEL // bn,),
        in_specs=[
            pl.BlockSpec((rows, D_MODEL), lambda j: (0, 0)),
            pl.BlockSpec((D_MODEL, bn), lambda j: (0, j)),
            pl.BlockSpec((1, bn), lambda j: (0, j)),
        ],
        out_specs=pl.BlockSpec((rows, bn), lambda j: (0, j)),
        out_shape=jax.ShapeDtypeStruct((rows, 6 * D_MODEL), F32),
        compiler_params=pltpu.CompilerParams(dimension_semantics=("arbitrary",), vmem_limit_bytes=VMEM_LIMIT),
        name="mod",
    )(c_all, w_ada, b_ada)


def _mod_specs_prompt(tm, k):
    tiles_per_seq = SEQ // tm
    return pl.BlockSpec((None, None, 1, D_MODEL), lambda i, *_: (i // tiles_per_seq, k, 0, 0))


def _mod_specs_sample(k):
    return pl.BlockSpec((DEC_BATCH, D_MODEL), lambda i, *_: (0, k))


def _inproj_kernel(x_ref, sh_ref, sc_ref, g_ref, w_ref, qt_ref, kt_ref, vt_ref, u_ref):
    h = _rms(x_ref[...], g_ref[...]) * (1.0 + sc_ref[...]) + sh_ref[...]
    hb = h.astype(BF16)

    def proj(i):
        return jnp.dot(hb, w_ref[:, i * D_ATTN:(i + 1) * D_ATTN], preferred_element_type=F32)

    qt_ref[...] = proj(0).T
    kt_ref[...] = proj(1).T
    vt_ref[...] = proj(2).T
    a = proj(3)
    g = proj(4)
    u_ref[...] = a * jax.nn.sigmoid(g)


def _inproj(x2d, mod, mod_spec, g_mix, w_in_b, tm, t_shape, t_spec):
    rows = x2d.shape[0]
    row_spec = lambda width: pl.BlockSpec((tm, width), lambda i: (i, 0))
    t_out = jax.ShapeDtypeStruct(t_shape, F32)
    return pl.pallas_call(
        _inproj_kernel,
        grid=(rows // tm,),
        in_specs=[
            row_spec(D_MODEL),
            mod_spec(0),
            mod_spec(1),
            pl.BlockSpec((1, D_MODEL), lambda i: (0, 0)),
            pl.BlockSpec((D_MODEL, D_IN), lambda i: (0, 0)),
        ],
        out_specs=[t_spec, t_spec, t_spec, row_spec(D_CONV)],
        out_shape=[t_out, t_out, t_out, jax.ShapeDtypeStruct((rows, D_CONV), F32)],
        compiler_params=pltpu.CompilerParams(dimension_semantics=("arbitrary",), vmem_limit_bytes=VMEM_LIMIT),
        name="inproj",
    )(x2d, mod, mod, g_mix, w_in_b)


CONV_TM = 256
CONV_HALO = 32


def _conv_post(y, lng, lnb, gco):
    mu = jnp.mean(y, axis=-1, keepdims=True)
    yc = y - mu
    yn = yc * lax.rsqrt(jnp.mean(yc * yc, axis=-1, keepdims=True) + EPS) * lng + lnb
    s = yn * jax.nn.sigmoid(yn)
    return _rms(s, gco)


CONV_SPAN = CONV_HALO + CONV_TM - SUBLANES


def _conv_kernel(prev_ref, cur_ref, wdw_ref, bdw_ref, lng_ref, lnb_ref, gco_ref, o_ref, sh_ref):
    i = pl.program_id(1)
    prev = jnp.where(i == 0, 0.0, prev_ref[...])
    win = jnp.concatenate([prev, cur_ref[...]], axis=0)
    first = CONV_HALO - (CONV_WIDTH - 1)
    acc = jnp.zeros((CONV_TM, D_CONV), F32)
    for r in range(SUBLANES):
        offs = [o for o in range(first, first + CONV_WIDTH) if o % SUBLANES == r]
        if r > 0:
            sh_ref[r - 1] = win[r:r + CONV_SPAN, :]
        for o in offs:
            a8 = o - r
            tap = win[a8:a8 + CONV_TM, :] if r == 0 else sh_ref[r - 1, a8:a8 + CONV_TM, :]
            acc = acc + tap * wdw_ref[o - first:o - first + 1, :]
    y = acc + bdw_ref[...]
    o_ref[...] = _conv_post(y, lng_ref[...], lnb_ref[...], gco_ref[...]).astype(BF16)


def _conv_prompt(u3, w_dw, b_dw, ln_g, ln_b, g_co):
    halo_per_tile = CONV_TM // CONV_HALO
    vec = pl.BlockSpec((1, D_CONV), lambda b, i: (0, 0))
    return pl.pallas_call(
        _conv_kernel,
        grid=(BATCH, SEQ // CONV_TM),
        in_specs=[
            pl.BlockSpec((None, CONV_HALO, D_CONV), lambda b, i: (b, jnp.maximum(i * halo_per_tile - 1, 0), 0)),
            pl.BlockSpec((None, CONV_TM, D_CONV), lambda b, i: (b, i, 0)),
            pl.BlockSpec((CONV_WIDTH, D_CONV), lambda b, i: (0, 0)),
            vec, vec, vec, vec,
        ],
        out_specs=pl.BlockSpec((None, CONV_TM, D_CONV), lambda b, i: (b, i, 0)),
        out_shape=jax.ShapeDtypeStruct((BATCH, SEQ, D_CONV), BF16),
        scratch_shapes=[pltpu.VMEM((SUBLANES - 1, CONV_SPAN, D_CONV), F32)],
        compiler_params=pltpu.CompilerParams(dimension_semantics=("arbitrary", "arbitrary"),
                                             vmem_limit_bytes=VMEM_LIMIT),
        name="conv_prompt",
    )(u3, u3, w_dw, b_dw, ln_g, ln_b, g_co)


SCONV_TN = 32
N_HIST = CONV_WIDTH - 1


def _sconv_kernel(hist_ref, u_ref, wdw_ref, bdw_ref, lng_ref, lnb_ref, gco_ref, o_ref, nh_ref):
    u = u_ref[...]
    y = u * wdw_ref[N_HIST:CONV_WIDTH, :] + bdw_ref[...]
    for w in range(N_HIST):
        y = y + hist_ref[w] * wdw_ref[w:w + 1, :]
    o_ref[...] = _conv_post(y, lng_ref[...], lnb_ref[...], gco_ref[...]).astype(BF16)
    for w in range(N_HIST - 1):
        nh_ref[w] = hist_ref[w + 1]
    nh_ref[N_HIST - 1] = u


def _conv_sample(hist_t, u, w_dw, b_dw, ln_g, ln_b, g_co):
    vec = pl.BlockSpec((1, D_CONV), lambda i: (0, 0))
    hist_spec = pl.BlockSpec((N_HIST, SCONV_TN, D_CONV), lambda i: (0, i, 0))
    return pl.pallas_call(
        _sconv_kernel,
        grid=(DEC_BATCH // SCONV_TN,),
        in_specs=[
            hist_spec,
            pl.BlockSpec((SCONV_TN, D_CONV), lambda i: (i, 0)),
            pl.BlockSpec((CONV_WIDTH, D_CONV), lambda i: (0, 0)),
            vec, vec, vec, vec,
        ],
        out_specs=[pl.BlockSpec((SCONV_TN, D_CONV), lambda i: (i, 0)), hist_spec],
        out_shape=[jax.ShapeDtypeStruct((DEC_BATCH, D_CONV), BF16),
                   jax.ShapeDtypeStruct((N_HIST, DEC_BATCH, D_CONV), F32)],
        compiler_params=pltpu.CompilerParams(dimension_semantics=("arbitrary",), vmem_limit_bytes=VMEM_LIMIT),
        name="conv_sample",
    )(hist_t, u, w_dw, b_dw, ln_g, ln_b, g_co)


PAIR = 2 * HEAD_DIM


def _moba_kernel(rb_ref, qt_ref, kt_ref, vt_ref, o_ref, kb_ref, vte_ref, mask_ref, bias_ref, s_ref, p_ref, ot_ref):
    b = pl.program_id(0)
    hp = pl.program_id(1)
    blk = MOBA_BLOCK
    nb = N_PROMPT_BLOCKS
    drow = lax.broadcasted_iota(jnp.int32, (PAIR, 1), 0)
    head_rows = [(drow >= e * HEAD_DIM) & (drow < (e + 1) * HEAD_DIM) for e in range(2)]

    @pl.when(b == 0)
    def _():
        kk = lax.broadcasted_iota(jnp.int32, (blk, blk), 0)
        qq = lax.broadcasted_iota(jnp.int32, (blk, blk), 1)
        d0 = qq - kk
        for e in range(2):
            h = 2 * hp + e
            bias_ref[h, 0] = jnp.where(d0 >= 0, _bias_of_distance(jnp.maximum(d0, 0), rb_ref, h) * LOG2E, NEG)
            bias_ref[h, 1] = _bias_of_distance(d0 + blk, rb_ref, h) * LOG2E

    km_cols = []
    for j in range(nb):
        ktj = kt_ref[:, j * blk:(j + 1) * blk]
        kb_ref[j * blk:(j + 1) * blk, :] = ktj.T.astype(BF16)
        km_cols.append(jnp.sum(ktj, axis=1, keepdims=True) * (1.0 / blk))
    vt = vt_ref[...]
    for e in range(2):
        vte_ref[e] = jnp.where(head_rows[e], vt, 1.0).astype(BF16)

    qt = qt_ref[...]
    qblk = lax.broadcasted_iota(jnp.int32, (1, SEQ), 1) // blk
    for e in range(2):
        h = 2 * hp + e
        far2 = rb_ref[N_BUCKETS - 1, h] * LOG2E
        qh = qt[e * HEAD_DIM:(e + 1) * HEAD_DIM, :]
        gates = [jnp.sum(qh * km_cols[j][e * HEAD_DIM:(e + 1) * HEAD_DIM, :], axis=0, keepdims=True)
                 for j in range(nb - 1)]
        gate = jnp.concatenate(gates + [jnp.zeros((1, SEQ), F32)], axis=0)
        row = lax.broadcasted_iota(jnp.int32, (nb, SEQ), 0)
        cnt = jnp.zeros((nb, SEQ), jnp.int32)
        for i in range(nb - 1):
            beats = jnp.where(gates[i] > gate, 1, jnp.where((gates[i] == gate) & (i < row), 1, 0))
            cnt = cnt + jnp.where(i < qblk, beats, 0)
        keep = jnp.where(row < qblk, jnp.where(cnt < MOBA_TOPK, 0.0, NEG), NEG)
        mask_ref[e] = keep + far2

    qscale = (HEAD_DIM ** -0.5) * LOG2E
    for qi in range(nb):
        qs = slice(qi * blk, (qi + 1) * blk)
        keys = (qi + 1) * blk
        qtile = qt_ref[:, qs] * qscale
        qcat = jnp.concatenate([jnp.where(head_rows[e], qtile, 0.0) for e in range(2)], axis=1).astype(BF16)
        s_ref[0:keys, :] = jnp.dot(kb_ref[0:keys, :], qcat, preferred_element_type=F32)
        shifts = []
        for e in range(2):
            h = 2 * hp + e
            far2 = rb_ref[N_BUCKETS - 1, h] * LOG2E
            cs = slice(e * blk, (e + 1) * blk)
            rows = []
            tops = []
            for j in range(qi + 1):
                ks = slice(j * blk, (j + 1) * blk)
                if j == qi:
                    sj = s_ref[ks, cs] + bias_ref[h, 0]
                    s_ref[ks, cs] = sj
                    row = None
                elif j == qi - 1:
                    sj = s_ref[ks, cs] + bias_ref[h, 1]
                    s_ref[ks, cs] = sj
                    row = mask_ref[e, j:j + 1, qs] - far2
                else:
                    sj = s_ref[ks, cs]
                    row = mask_ref[e, j:j + 1, qs]
                top = jnp.max(sj, axis=0, keepdims=True)
                rows.append(row)
                tops.append(top if row is None else top + row)
            m = functools.reduce(jnp.maximum, tops)
            shifts.append([m if row is None else m - row for row in rows])
        for e in range(2):
            cs = slice(e * blk, (e + 1) * blk)
            for j in range(qi + 1):
                ks = slice(j * blk, (j + 1) * blk)
                p_ref[ks, cs] = jnp.exp2(s_ref[ks, cs] - shifts[e][j]).astype(BF16)
        for e in range(2):
            cs = slice(e * blk, (e + 1) * blk)
            ot = jnp.dot(vte_ref[e, :, 0:keys], p_ref[0:keys, cs], preferred_element_type=F32)
            denom = ot[(1 - e) * HEAD_DIM:(1 - e) * HEAD_DIM + 1, :]
            ot_ref[e * HEAD_DIM:(e + 1) * HEAD_DIM, qs] = ot[e * HEAD_DIM:(e + 1) * HEAD_DIM, :] / denom
    o_ref[...] = ot_ref[...].T


def _moba_prompt(rel_bias, qt3, kt3, vt3):
    slab = pl.BlockSpec((None, PAIR, SEQ), lambda b, hp: (b, hp, 0))
    return pl.pallas_call(
        _moba_kernel,
        grid=(BATCH, N_HEADS // 2),
        in_specs=[pl.BlockSpec(memory_space=pltpu.SMEM), slab, slab, slab],
        out_specs=pl.BlockSpec((None, SEQ, PAIR), lambda b, hp: (b, 0, hp)),
        out_shape=jax.ShapeDtypeStruct((BATCH, SEQ, D_ATTN), F32),
        scratch_shapes=[
            pltpu.VMEM((SEQ, PAIR), BF16),
            pltpu.VMEM((2, PAIR, SEQ), BF16),
            pltpu.VMEM((2, N_PROMPT_BLOCKS, SEQ), F32),
            pltpu.VMEM((N_HEADS, 2, MOBA_BLOCK, MOBA_BLOCK), F32),
            pltpu.VMEM((SEQ, 2 * MOBA_BLOCK), F32),
            pltpu.VMEM((SEQ, 2 * MOBA_BLOCK), BF16),
            pltpu.VMEM((PAIR, SEQ), F32),
        ],
        compiler_params=pltpu.CompilerParams(dimension_semantics=("arbitrary", "arbitrary"),
                                             vmem_limit_bytes=VMEM_LIMIT),
        name="moba_prompt",
    )(rel_bias, qt3, kt3, vt3)


def _column(mat, n):
    lane = lax.broadcasted_iota(jnp.int32, mat.shape, 1)
    return jnp.sum(jnp.where(lane == n, mat, 0.0), axis=1, keepdims=True)


def _scan_pages(page, n_pages, qb_ref, store_row, g, first_block):
    lane = lax.broadcasted_iota(jnp.int32, (N_HEADS, LANES), 1)
    sub = lax.broadcasted_iota(jnp.int32, (N_HEADS, LANES), 0)
    for bi in range(n_pages // PAGES_PER_BLOCK):
        tile = jnp.zeros((N_HEADS, LANES), F32)
        for h in range(N_HEADS):
            qh = qb_ref[h * HEAD_DIM:(h + 1) * HEAD_DIM, :]
            block_row = None
            for half in range(PAGES_PER_BLOCK):
                i = bi * PAGES_PER_BLOCK + half
                row = jnp.sum(page(i)[h] * qh, axis=0, keepdims=True)
                store_row(h, i, row)
                block_row = row if block_row is None else block_row + row
            tile = jnp.where(sub == h, block_row, tile)
        gcol = jnp.sum(tile, axis=1, keepdims=True) * (1.0 / MOBA_BLOCK)
        g = jnp.where(lane == first_block + bi, gcol, g)
    return g


def _top_blocks(g):
    lane = lax.broadcasted_iota(jnp.int32, (N_HEADS, LANES), 1)
    lane_f = lane.astype(F32)
    out = jnp.zeros((N_HEADS, LANES), jnp.int32)
    for r in range(MOBA_TOPK):
        best = jnp.max(g, axis=1, keepdims=True)
        pick = jnp.min(jnp.where(g == best, lane_f, float(LANES)), axis=1, keepdims=True)
        out = jnp.where(lane == r, pick.astype(jnp.int32), out)
        g = jnp.where(lane_f == pick, -jnp.inf, g)
    return out


SLABS = MOBA_TOPK * PAGES_PER_BLOCK
assert SLABS + 1 <= SUBLANES


def _sattn_copies(sel_ref, pt_ref, cv_hbm, vbuf, sem, n, slot):
    out = []
    for h in range(N_HEADS):
        for t in range(MOBA_TOPK):
            blk = sel_ref[(n * N_HEADS + h) * MOBA_TOPK + t]
            for half in range(PAGES_PER_BLOCK):
                page = pt_ref[n * N_PAGES + blk * PAGES_PER_BLOCK + half]
                out.append(pltpu.make_async_copy(cv_hbm.at[page, h], vbuf.at[slot, h, t * PAGES_PER_BLOCK + half],
                                                 sem.at[slot]))
    return out


def _sattn_kernel(sel_ref, pt_ref, rb_ref, lg_ref, qt_ref, kt_ref, vt_ref, cv_hbm, o_ref,
                  vbuf, tab_ref, self_ref, acc_ref, w_ref, sem):
    n = pl.program_id(0)
    slot = lax.rem(n, 2)
    copies = functools.partial(_sattn_copies, sel_ref, pt_ref, cv_hbm, vbuf, sem)
    scale = HEAD_DIM ** -0.5
    lane = lax.broadcasted_iota(jnp.int32, (SUBLANES, LANES), 1)
    sub = lax.broadcasted_iota(jnp.int32, (SUBLANES, LANES), 0)

    def start_all(cs):
        for i, c in enumerate(cs):
            c.start(priority=i % N_DMA_THREADS)

    @pl.when(n == 0)
    def _():
        start_all(copies(0, 0))
        dist = MOBA_BLOCK - lax.broadcasted_iota(jnp.int32, (1, MOBA_BLOCK), 1)
        prod = qt_ref[...] * kt_ref[...]
        for h in range(N_HEADS):
            tab_ref[h:h + 1, :] = _bias_of_distance(dist, rb_ref, h)
            self_ref[h:h + 1, :] = (jnp.sum(prod[h * HEAD_DIM:(h + 1) * HEAD_DIM, :], axis=0, keepdims=True) * scale
                                    + rb_ref[0, h])
        w_ref[...] = jnp.zeros((N_HEADS, LANES), F32)
        acc_ref[...] = jnp.zeros((D_ATTN, DEC_BATCH), F32)

    @pl.when(n + 1 < DEC_BATCH)
    def _():
        start_all(copies(n + 1, 1 - slot))

    for c in copies(n, slot):
        c.wait()

    self_col = _column(self_ref[...], n)
    w_tile = w_ref[...]
    lane_wide = lax.broadcasted_iota(jnp.int32, (HEAD_DIM, LANES), 1)
    for h in range(N_HEADS):
        far_bias = rb_ref[N_BUCKETS - 1, h]
        s = jnp.full((SUBLANES, LANES), NEG, F32)
        for t in range(MOBA_TOPK):
            blk = sel_ref[(n * N_HEADS + h) * MOBA_TOPK + t]
            for half in range(PAGES_PER_BLOCK):
                r = t * PAGES_PER_BLOCK + half
                row = lg_ref[h, pl.ds(blk * PAGES_PER_BLOCK + half, 1), :] * scale
                bias = jnp.where(blk == N_PAST_BLOCKS - 1, tab_ref[h:h + 1, half * PAGE_SIZE:(half + 1) * PAGE_SIZE],
                                 far_bias)
                s = jnp.where(sub == r, row + bias, s)
        s_new = self_col[h:h + 1, :]
        m = jnp.maximum(jnp.max(jnp.max(s, axis=1, keepdims=True), axis=0, keepdims=True), s_new)
        p = jnp.exp(s - m)
        p_new = jnp.exp(s_new - m)
        l = jnp.sum(jnp.sum(p, axis=1, keepdims=True), axis=0, keepdims=True) + p_new
        acc = jnp.zeros((HEAD_DIM, LANES), F32)
        for r in range(SLABS):
            acc = acc + vbuf[slot, h, r] * p[r:r + 1, :]
        col = jnp.sum(acc, axis=1, keepdims=True) / l
        rows = slice(h * HEAD_DIM, (h + 1) * HEAD_DIM)
        acc_ref[rows, :] = jnp.where(lane_wide == n, col, acc_ref[rows, :])
        w_tile = jnp.where((sub == h) & (lane == n), p_new / l, w_tile)
    w_ref[...] = w_tile

    @pl.when(n == DEC_BATCH - 1)
    def _():
        for h in range(N_HEADS):
            rows = slice(h * HEAD_DIM, (h + 1) * HEAD_DIM)
            o_ref[rows, :] = acc_ref[rows, :] + w_tile[h:h + 1, :] * vt_ref[rows, :]


def _attn_sample(sel_flat, pt_flat, rel_bias, logits, qst, kst, vst, cache_vt):
    full = pl.BlockSpec((D_ATTN, DEC_BATCH), lambda n, sel, pt: (0, 0))
    grid_spec = pltpu.PrefetchScalarGridSpec(
        num_scalar_prefetch=2,
        grid=(DEC_BATCH,),
        in_specs=[
            pl.BlockSpec(memory_space=pltpu.SMEM),
            pl.BlockSpec((None, N_HEADS, N_PAGES, PAGE_SIZE), lambda n, sel, pt: (n, 0, 0, 0)),
            full, full, full,
            pl.BlockSpec(memory_space=pl.ANY),
        ],
        out_specs=full,
        scratch_shapes=[
            pltpu.VMEM((2, N_HEADS, SLABS, HEAD_DIM, PAGE_SIZE), F32),
            pltpu.VMEM((N_HEADS, MOBA_BLOCK), F32),
            pltpu.VMEM((N_HEADS, DEC_BATCH), F32),
            pltpu.VMEM((D_ATTN, DEC_BATCH), F32),
            pltpu.VMEM((N_HEADS, DEC_BATCH), F32),
            pltpu.SemaphoreType.DMA((2,)),
        ],
    )
    return pl.pallas_call(
        _sattn_kernel,
        grid_spec=grid_spec,
        out_shape=jax.ShapeDtypeStruct((D_ATTN, DEC_BATCH), F32),
        compiler_params=pltpu.CompilerParams(dimension_semantics=("arbitrary",), vmem_limit_bytes=VMEM_LIMIT),
        name="attn_sample",
    )(sel_flat, pt_flat, rel_bias, logits, qst, kst, vst, cache_vt)


FF_CHUNK = 1024
N_FF_CHUNKS = D_FF // FF_CHUNK


def _ffn_head(x_ref, attn_ref, conv_ref, gt1_ref, sh2_ref, sc2_ref, ga_ref, gffn_ref, wo_ref):
    attn_n = _rms(attn_ref[...], ga_ref[...]).astype(BF16)
    mixed = (jnp.dot(attn_n, wo_ref[:D_ATTN, :], preferred_element_type=F32)
             + jnp.dot(conv_ref[...], wo_ref[D_ATTN:, :], preferred_element_type=F32))
    x1 = x_ref[...] + gt1_ref[...] * mixed
    h2 = (_rms(x1, gffn_ref[...]) * (1.0 + sc2_ref[...]) + sh2_ref[...]).astype(BF16)
    return x1, h2


def _ffn_chunk(h2, w1_ref, w2_ref, c):
    f = jnp.dot(h2, w1_ref[:, c * FF_CHUNK:(c + 1) * FF_CHUNK], preferred_element_type=F32)
    f = jnp.square(jnp.maximum(f, 0.0)).astype(BF16)
    return jnp.dot(f, w2_ref[c * FF_CHUNK:(c + 1) * FF_CHUNK, :], preferred_element_type=F32)


def _ffn_kernel(x_ref, attn_ref, conv_ref, gt1_ref, sh2_ref, sc2_ref, gt2_ref, ga_ref, gffn_ref, gfin_ref,
                wo_ref, w1_ref, w2_ref, y_ref):
    x1, h2 = _ffn_head(x_ref, attn_ref, conv_ref, gt1_ref, sh2_ref, sc2_ref, ga_ref, gffn_ref, wo_ref)
    acc = _ffn_chunk(h2, w1_ref, w2_ref, 0)
    for c in range(1, N_FF_CHUNKS):
        acc = acc + _ffn_chunk(h2, w1_ref, w2_ref, c)
    y_ref[...] = _rms(x1 + gt2_ref[...] * acc, gfin_ref[...])


def _ffn_in_specs(mod_spec, tm):
    row_spec = lambda width: pl.BlockSpec((tm, width), lambda i, *_: (i, 0))
    const = lambda shape: pl.BlockSpec(shape, lambda i, *_: (0, 0), pipeline_mode=pl.Buffered(1))
    return [
        row_spec(D_MODEL), row_spec(D_ATTN), row_spec(D_CONV),
        mod_spec(2), mod_spec(3), mod_spec(4), mod_spec(5),
        const((1, D_ATTN)), const((1, D_MODEL)), const((1, D_MODEL)),
        const((D_MODEL, D_MODEL)), const((D_MODEL, D_FF)), const((D_FF, D_MODEL)),
    ]


def _ffn(x2d, attn2d, conv2d, mod, mod_spec, g_attn, g_ffn, g_final, wo_b, w1_b, w2_b, tm):
    rows = x2d.shape[0]
    return pl.pallas_call(
        _ffn_kernel,
        grid=(rows // tm,),
        in_specs=_ffn_in_specs(mod_spec, tm),
        out_specs=pl.BlockSpec((tm, D_MODEL), lambda i: (i, 0)),
        out_shape=jax.ShapeDtypeStruct((rows, D_MODEL), F32),
        compiler_params=pltpu.CompilerParams(dimension_semantics=("arbitrary",), vmem_limit_bytes=VMEM_LIMIT),
        name="ffn",
    )(x2d, attn2d, conv2d, mod, mod, mod, mod, g_attn, g_ffn, g_final, wo_b, w1_b, w2_b)


SCAN_TM = 256
SCAN_REGION_PAGES = 32
REGIONS_PER_SEQ = N_PAGES // SCAN_REGION_PAGES
SCAN_SEQS = N_FF_CHUNKS // REGIONS_PER_SEQ
SCAN_GRID = BATCH * SEQ // SCAN_TM
REGION_BLOCKS = SCAN_REGION_PAGES // PAGES_PER_BLOCK
assert SCAN_GRID * SCAN_SEQS == DEC_BATCH and N_FF_CHUNKS % 2 == 0 and N_FF_CHUNKS % REGIONS_PER_SEQ == 0


def _region_copies(pt_ref, ck_hbm, kbuf, sem, region, slot):
    return [pltpu.make_async_copy(ck_hbm.at[pt_ref[region * SCAN_REGION_PAGES + i]], kbuf.at[slot, i], sem.at[slot])
            for i in range(SCAN_REGION_PAGES)]


def _ffn_scan_kernel(pt_ref, x_ref, attn_ref, conv_ref, gt1_ref, sh2_ref, sc2_ref, gt2_ref, ga_ref, gffn_ref,
                     gfin_ref, wo_ref, w1_ref, w2_ref, qst_ref, ck_hbm, y_ref, lg_ref, idx_ref, kbuf, qb_ref, sem):
    step = pl.program_id(0)
    last_region = SCAN_GRID * N_FF_CHUNKS - 1
    copies = functools.partial(_region_copies, pt_ref, ck_hbm, kbuf, sem)

    def start_all(cs):
        for i, c in enumerate(cs):
            c.start(priority=i % N_DMA_THREADS)

    @pl.when(step == 0)
    def _():
        start_all(copies(0, 0))

    x1 = h2 = acc = g = None
    for r in range(N_FF_CHUNKS):
        region = step * N_FF_CHUNKS + r
        slot = r % 2
        start_all(copies(jnp.minimum(region + 1, last_region), 1 - slot))
        for c in copies(region, slot):
            c.wait()

        seq_local, part = divmod(r, REGIONS_PER_SEQ)
        if part == 0:
            n = step * SCAN_SEQS + seq_local
            qb_ref[...] = jnp.broadcast_to(_column(qst_ref[...], n), (D_ATTN, LANES))
            g = jnp.full((N_HEADS, LANES), -jnp.inf, F32)

        def store_row(h, i, row, seq_local=seq_local, part=part):
            pg = part * SCAN_REGION_PAGES + i
            lg_ref[seq_local, h, pg:pg + 1, :] = row

        g = _scan_pages(lambda i, slot=slot: kbuf.at[slot, i], SCAN_REGION_PAGES, qb_ref, store_row, g,
                        part * REGION_BLOCKS)
        if part == REGIONS_PER_SEQ - 1:
            idx_ref[seq_local] = _top_blocks(g)

        if r == 0:
            x1, h2 = _ffn_head(x_ref, attn_ref, conv_ref, gt1_ref, sh2_ref, sc2_ref, ga_ref, gffn_ref, wo_ref)
            acc = _ffn_chunk(h2, w1_ref, w2_ref, 0)
        else:
            acc = acc + _ffn_chunk(h2, w1_ref, w2_ref, r)
    y_ref[...] = _rms(x1 + gt2_ref[...] * acc, gfin_ref[...])

    @pl.when(step == SCAN_GRID - 1)
    def _():
        for c in copies(last_region, 0):
            c.wait()


def _ffn_scan(pt_flat, x2d, attn2d, conv2d, mod, mod_spec, g_attn, g_ffn, g_final, wo_b, w1_b, w2_b, qst, cache_kt):
    grid_spec = pltpu.PrefetchScalarGridSpec(
        num_scalar_prefetch=1,
        grid=(SCAN_GRID,),
        in_specs=_ffn_in_specs(mod_spec, SCAN_TM) + [
            pl.BlockSpec((D_ATTN, DEC_BATCH), lambda i, pt: (0, 0)),
            pl.BlockSpec(memory_space=pl.ANY),
        ],
        out_specs=[
            pl.BlockSpec((SCAN_TM, D_MODEL), lambda i, pt: (i, 0)),
            pl.BlockSpec((SCAN_SEQS, N_HEADS, N_PAGES, PAGE_SIZE), lambda i, pt: (i, 0, 0, 0)),
            pl.BlockSpec((SCAN_SEQS, N_HEADS, LANES), lambda i, pt: (i, 0, 0)),
        ],
        scratch_shapes=[
            pltpu.VMEM((2, SCAN_REGION_PAGES, N_HEADS, HEAD_DIM, PAGE_SIZE), F32),
            pltpu.VMEM((D_ATTN, LANES), F32),
            pltpu.SemaphoreType.DMA((2,)),
        ],
    )
    return pl.pallas_call(
        _ffn_scan_kernel,
        grid_spec=grid_spec,
        out_shape=[jax.ShapeDtypeStruct((BATCH * SEQ, D_MODEL), F32),
                   jax.ShapeDtypeStruct((DEC_BATCH, N_HEADS, N_PAGES, PAGE_SIZE), F32),
                   jax.ShapeDtypeStruct((DEC_BATCH, N_HEADS, LANES), jnp.int32)],
        compiler_params=pltpu.CompilerParams(dimension_semantics=("arbitrary",), vmem_limit_bytes=VMEM_LIMIT),
        name="ffn_scan",
    )(pt_flat, x2d, attn2d, conv2d, mod, mod, mod, mod, g_attn, g_ffn, g_final, wo_b, w1_b, w2_b, qst, cache_kt)


def kernel(x_prompt, x_sample, c_prompt, c_sample, cache_k, cache_v, state_conv, page_table, rel_bias, w_ada, b_ada, g_mix, w_in, w_dw, b_dw, ln_conv_g, ln_conv_b, g_attn_out, g_conv_out, w_out, g_ffn, w_ff1, w_ff2, g_final):
    w_in_b = w_in[0].astype(BF16)
    wo_b = w_out[0].astype(BF16)
    w1_b = w_ff1[0].astype(BF16)
    w2_b = w_ff2[0].astype(BF16)
    g_fin = g_final.reshape(1, D_MODEL)

    mod = _mod(jnp.concatenate([c_prompt, c_sample], axis=0), w_ada[0], b_ada)
    mod_p = mod[:BATCH].reshape(BATCH, 6, 1, D_MODEL)
    mod_s = mod[BATCH:]

    tm = 512
    tiles_per_seq = SEQ // tm
    xp = x_prompt.reshape(BATCH * SEQ, D_MODEL)
    qt, kt, vt, u = _inproj(
        xp, mod_p, functools.partial(_mod_specs_prompt, tm), g_mix, w_in_b, tm, (BATCH, D_ATTN, SEQ),
        pl.BlockSpec((None, D_ATTN, tm), lambda i: (i // tiles_per_seq, 0, i % tiles_per_seq)))
    u3 = u.reshape(BATCH, SEQ, D_CONV)
    conv_n = _conv_prompt(u3, w_dw[0], b_dw, ln_conv_g, ln_conv_b, g_conv_out)
    attn = _moba_prompt(rel_bias, qt, kt, vt)

    xs = x_sample.reshape(DEC_BATCH, D_MODEL)
    qst, kst, vst, us = _inproj(xs, mod_s, _mod_specs_sample, g_mix, w_in_b, DEC_BATCH, (D_ATTN, DEC_BATCH),
                                pl.BlockSpec((D_ATTN, DEC_BATCH), lambda i: (0, 0)))
    hist_t = jnp.transpose(state_conv[0], (1, 0, 2))
    conv_s, new_hist_t = _conv_sample(hist_t, us, w_dw[0], b_dw, ln_conv_g, ln_conv_b, g_conv_out)
    cache_kt = jnp.transpose(cache_k[0], (0, 2, 3, 1))
    cache_vt = jnp.transpose(cache_v[0], (0, 2, 3, 1))
    pt_flat = page_table.reshape(-1)
    y_p, logits, sel = _ffn_scan(
        pt_flat, xp, attn.reshape(BATCH * SEQ, D_ATTN), conv_n.reshape(BATCH * SEQ, D_CONV), mod_p,
        functools.partial(_mod_specs_prompt, SCAN_TM), g_attn_out, g_ffn, g_fin, wo_b, w1_b, w2_b, qst, cache_kt)
    sel_flat = sel[:, :, :MOBA_TOPK].reshape(-1)
    attn_st = _attn_sample(sel_flat, pt_flat, rel_bias, logits, qst, kst, vst, cache_vt)
    y_s = _ffn(xs, attn_st.T, conv_s, mod_s, _mod_specs_sample,
               g_attn_out, g_ffn, g_fin, wo_b, w1_b, w2_b, DEC_BATCH)

    kv_p = lambda a: jnp.transpose(a.reshape(1, BATCH, N_HEADS, HEAD_DIM, SEQ), (0, 1, 4, 2, 3))
    kv_s = lambda a: jnp.transpose(a.reshape(1, 1, N_HEADS, HEAD_DIM, DEC_BATCH), (0, 4, 1, 2, 3))
    hist_p = u3[:, SEQ - N_HIST:, :][None]
    hist_s = jnp.transpose(new_hist_t, (1, 0, 2))[None]
    return (y_p.reshape(BATCH, SEQ, D_MODEL), y_s.reshape(DEC_BATCH, 1, D_MODEL),
            kv_p(kt), kv_p(vt), hist_p, kv_s(kst), kv_s(vst), hist_s)
```

```python
import functools
import math

import numpy as np
import jax
import jax.numpy as jnp
from jax import lax
from jax.experimental import pallas as pl
from jax.experimental.pallas import tpu as pltpu

D_MODEL = 1024
BATCH = 8
SEQ = 2048
DEC_BATCH = 128
PAST_LEN = 8192
PAGE_SIZE = 128
D_ATTN = 512
D_CONV = 512
HEAD_DIM = 64
N_HEADS = 8
CONV_WIDTH = 31
MOBA_BLOCK = 256
MOBA_TOPK = 3
N_BUCKETS = 32
MAX_DISTANCE = 128
D_FF = 4096
EPS = 1e-6
D_IN = 3 * D_ATTN + 2 * D_CONV
N_PAGES = PAST_LEN // PAGE_SIZE
N_PAST_BLOCKS = PAST_LEN // MOBA_BLOCK
PAGES_PER_BLOCK = MOBA_BLOCK // PAGE_SIZE
N_PROMPT_BLOCKS = SEQ // MOBA_BLOCK
LANES = 128
SUBLANES = 8

F32 = jnp.float32
BF16 = jnp.bfloat16
NEG = -1e30
LOG2E = math.log2(math.e)
VMEM_LIMIT = 56 * 1024 * 1024


def _bucket_thresholds():
    n = np.arange(0, 4 * MAX_DISTANCE)
    max_exact = N_BUCKETS // 2
    ratio = np.maximum(n, max_exact).astype(np.float32) / np.float32(max_exact)
    val = np.log(ratio) / np.float32(math.log(MAX_DISTANCE / max_exact)) * np.float32(N_BUCKETS - max_exact)
    large = np.minimum(max_exact + val.astype(np.int32), N_BUCKETS - 1)
    bucket = np.where(n < max_exact, n, large)
    assert np.all(np.diff(bucket) >= 0) and bucket[-1] == N_BUCKETS - 1
    return [int(np.argmax(bucket >= b)) for b in range(N_BUCKETS)]


BUCKET_START = _bucket_thresholds()
FAR_DISTANCE = BUCKET_START[N_BUCKETS - 1]
assert FAR_DISTANCE <= MOBA_BLOCK


def _rms(x, g):
    return x * lax.rsqrt(jnp.mean(x * x, axis=-1, keepdims=True) + EPS) * g


def _bias_of_distance(dist, rb_ref, h):
    val = jnp.full(dist.shape, rb_ref[0, h], F32)
    for b in range(1, N_BUCKETS):
        val = jnp.where(dist >= BUCKET_START[b], rb_ref[b, h], val)
    return val


def _mod_kernel(c_ref, w_ref, b_ref, o_ref):
    c = c_ref[...]
    s = (c * jax.nn.sigmoid(c)).astype(BF16)
    o_ref[...] = jnp.dot(s, w_ref[...].astype(BF16), preferred_element_type=F32) + b_ref[...]


def _mod(c_all, w_ada, b_ada):
    rows = c_all.shape[0]
    bn = 1024
    return pl.pallas_call(
        _mod_kernel,
        grid=(6 * D_MODEL // bn,),
        in_specs=[
            pl.BlockSpec((rows, D_MODEL), lambda j: (0, 0)),
            pl.BlockSpec((D_MODEL, bn), lambda j: (0, j)),
            pl.BlockSpec((1, bn), lambda j: (0, j)),
        ],
        out_specs=pl.BlockSpec((rows, bn), lambda j: (0, j)),
        out_shape=jax.ShapeDtypeStruct((rows, 6 * D_MODEL), F32),
        compiler_params=pltpu.CompilerParams(dimension_semantics=("arbitrary",), vmem_limit_bytes=VMEM_LIMIT),
        name="mod",
    )(c_all, w_ada, b_ada)


def _mod_specs_prompt(tm, k):
    tiles_per_seq = SEQ // tm
    return pl.BlockSpec((None, None, 1, D_MODEL), lambda i, *_: (i // tiles_per_seq, k, 0, 0))


def _mod_specs_sample(k):
    return pl.BlockSpec((DEC_BATCH, D_MODEL), lambda i, *_: (0, k))


def _inproj_kernel(x_ref, sh_ref, sc_ref, g_ref, w_ref, qt_ref, kt_ref, vt_ref, u_ref):
    h = _rms(x_ref[...], g_ref[...]) * (1.0 + sc_ref[...]) + sh_ref[...]
    hb = h.astype(BF16)

    def proj(i):
        return jnp.dot(hb, w_ref[:, i * D_ATTN:(i + 1) * D_ATTN], preferred_element_type=F32)

    qt_ref[...] = proj(0).T
    kt_ref[...] = proj(1).T
    vt_ref[...] = proj(2).T
    a = proj(3)
    g = proj(4)
    u_ref[...] = a * jax.nn.sigmoid(g)


def _inproj(x2d, mod, mod_spec, g_mix, w_in_b, tm, t_shape, t_spec):
    rows = x2d.shape[0]
    row_spec = lambda width: pl.BlockSpec((tm, width), lambda i: (i, 0))
    t_out = jax.ShapeDtypeStruct(t_shape, F32)
    return pl.pallas_call(
        _inproj_kernel,
        grid=(rows // tm,),
        in_specs=[
            row_spec(D_MODEL),
            mod_spec(0),
            mod_spec(1),
            pl.BlockSpec((1, D_MODEL), lambda i: (0, 0)),
            pl.BlockSpec((D_MODEL, D_IN), lambda i: (0, 0)),
        ],
        out_specs=[t_spec, t_spec, t_spec, row_spec(D_CONV)],
        out_shape=[t_out, t_out, t_out, jax.ShapeDtypeStruct((rows, D_CONV), F32)],
        compiler_params=pltpu.CompilerParams(dimension_semantics=("arbitrary",), vmem_limit_bytes=VMEM_LIMIT),
        name="inproj",
    )(x2d, mod, mod, g_mix, w_in_b)


CONV_TM = 256
CONV_HALO = 32


def _conv_post(y, lng, lnb, gco):
    mu = jnp.mean(y, axis=-1, keepdims=True)
    yc = y - mu
    yn = yc * lax.rsqrt(jnp.mean(yc * yc, axis=-1, keepdims=True) + EPS) * lng + lnb
    s = yn * jax.nn.sigmoid(yn)
    return _rms(s, gco)


CONV_SPAN = CONV_HALO + CONV_TM - SUBLANES


def _conv_window(win, wdw_ref, bdw_ref, lng_ref, lnb_ref, gco_ref, sh_ref):
    first = CONV_HALO - (CONV_WIDTH - 1)
    acc = jnp.zeros((CONV_TM, D_CONV), F32)
    for r in range(SUBLANES):
        offs = [o for o in range(first, first + CONV_WIDTH) if o % SUBLANES == r]
        if r > 0:
            sh_ref[r - 1] = win[r:r + CONV_SPAN, :]
        for o in offs:
            a8 = o - r
            tap = win[a8:a8 + CONV_TM, :] if r == 0 else sh_ref[r - 1, a8:a8 + CONV_TM, :]
            acc = acc + tap * wdw_ref[o - first:o - first + 1, :]
    y = acc + bdw_ref[...]
    return _conv_post(y, lng_ref[...], lnb_ref[...], gco_ref[...])


SCONV_TN = 32
N_HIST = CONV_WIDTH - 1


def _sconv_kernel(hist_ref, u_ref, wdw_ref, bdw_ref, lng_ref, lnb_ref, gco_ref, o_ref, nh_ref):
    u = u_ref[...]
    y = u * wdw_ref[N_HIST:CONV_WIDTH, :] + bdw_ref[...]
    for w in range(N_HIST):
        y = y + hist_ref[w] * wdw_ref[w:w + 1, :]
    o_ref[...] = _conv_post(y, lng_ref[...], lnb_ref[...], gco_ref[...]).astype(BF16)
    for w in range(N_HIST - 1):
        nh_ref[w] = hist_ref[w + 1]
    nh_ref[N_HIST - 1] = u


def _conv_sample(hist_t, u, w_dw, b_dw, ln_g, ln_b, g_co):
    vec = pl.BlockSpec((1, D_CONV), lambda i: (0, 0))
    hist_spec = pl.BlockSpec((N_HIST, SCONV_TN, D_CONV), lambda i: (0, i, 0))
    return pl.pallas_call(
        _sconv_kernel,
        grid=(DEC_BATCH // SCONV_TN,),
        in_specs=[
            hist_spec,
            pl.BlockSpec((SCONV_TN, D_CONV), lambda i: (i, 0)),
            pl.BlockSpec((CONV_WIDTH, D_CONV), lambda i: (0, 0)),
            vec, vec, vec, vec,
        ],
        out_specs=[pl.BlockSpec((SCONV_TN, D_CONV), lambda i: (i, 0)), hist_spec],
        out_shape=[jax.ShapeDtypeStruct((DEC_BATCH, D_CONV), BF16),
                   jax.ShapeDtypeStruct((N_HIST, DEC_BATCH, D_CONV), F32)],
        compiler_params=pltpu.CompilerParams(dimension_semantics=("arbitrary",), vmem_limit_bytes=VMEM_LIMIT),
        name="conv_sample",
    )(hist_t, u, w_dw, b_dw, ln_g, ln_b, g_co)


PAIR = 2 * HEAD_DIM


def _moba_kernel(rb_ref, qt_ref, kt_ref, vt_ref, o_ref, kb_ref, vte_ref, mask_ref, bias_ref, s_ref, p_ref, ot_ref):
    b = pl.program_id(0)
    hp = pl.program_id(1)
    blk = MOBA_BLOCK
    nb = N_PROMPT_BLOCKS
    drow = lax.broadcasted_iota(jnp.int32, (PAIR, 1), 0)
    head_rows = [(drow >= e * HEAD_DIM) & (drow < (e + 1) * HEAD_DIM) for e in range(2)]

    @pl.when(b == 0)
    def _():
        kk = lax.broadcasted_iota(jnp.int32, (blk, blk), 0)
        qq = lax.broadcasted_iota(jnp.int32, (blk, blk), 1)
        d0 = qq - kk
        for e in range(2):
            h = 2 * hp + e
            bias_ref[h, 0] = jnp.where(d0 >= 0, _bias_of_distance(jnp.maximum(d0, 0), rb_ref, h) * LOG2E, NEG)
            bias_ref[h, 1] = _bias_of_distance(d0 + blk, rb_ref, h) * LOG2E

    km_cols = []
    for j in range(nb):
        ktj = kt_ref[:, j * blk:(j + 1) * blk]
        kb_ref[j * blk:(j + 1) * blk, :] = ktj.T.astype(BF16)
        km_cols.append(jnp.sum(ktj, axis=1, keepdims=True) * (1.0 / blk))
    vt = vt_ref[...]
    for e in range(2):
        vte_ref[e] = jnp.where(head_rows[e], vt, 1.0).astype(BF16)

    qt = qt_ref[...]
    qblk = lax.broadcasted_iota(jnp.int32, (1, SEQ), 1) // blk
    for e in range(2):
        h = 2 * hp + e
        far2 = rb_ref[N_BUCKETS - 1, h] * LOG2E
        qh = qt[e * HEAD_DIM:(e + 1) * HEAD_DIM, :]
        gates = [jnp.sum(qh * km_cols[j][e * HEAD_DIM:(e + 1) * HEAD_DIM, :], axis=0, keepdims=True)
                 for j in range(nb - 1)]
        gate = jnp.concatenate(gates + [jnp.zeros((1, SEQ), F32)], axis=0)
        row = lax.broadcasted_iota(jnp.int32, (nb, SEQ), 0)
        cnt = jnp.zeros((nb, SEQ), jnp.int32)
        for i in range(nb - 1):
            beats = jnp.where(gates[i] > gate, 1, jnp.where((gates[i] == gate) & (i < row), 1, 0))
            cnt = cnt + jnp.where(i < qblk, beats, 0)
        keep = jnp.where(row < qblk, jnp.where(cnt < MOBA_TOPK, 0.0, NEG), NEG)
        mask_ref[e] = keep + far2

    qscale = (HEAD_DIM ** -0.5) * LOG2E
    for qi in range(nb):
        qs = slice(qi * blk, (qi + 1) * blk)
        keys = (qi + 1) * blk
        qtile = qt_ref[:, qs] * qscale
        qcat = jnp.concatenate([jnp.where(head_rows[e], qtile, 0.0) for e in range(2)], axis=1).astype(BF16)
        s_ref[0:keys, :] = jnp.dot(kb_ref[0:keys, :], qcat, preferred_element_type=F32)
        shifts = []
        for e in range(2):
            h = 2 * hp + e
            far2 = rb_ref[N_BUCKETS - 1, h] * LOG2E
            cs = slice(e * blk, (e + 1) * blk)
            rows = []
            tops = []
            for j in range(qi + 1):
                ks = slice(j * blk, (j + 1) * blk)
                if j == qi:
                    sj = s_ref[ks, cs] + bias_ref[h, 0]
                    s_ref[ks, cs] = sj
                    row = None
                elif j == qi - 1:
                    sj = s_ref[ks, cs] + bias_ref[h, 1]
                    s_ref[ks, cs] = sj
                    row = mask_ref[e, j:j + 1, qs] - far2
                else:
                    sj = s_ref[ks, cs]
                    row = mask_ref[e, j:j + 1, qs]
                top = jnp.max(sj, axis=0, keepdims=True)
                rows.append(row)
                tops.append(top if row is None else top + row)
            m = functools.reduce(jnp.maximum, tops)
            shifts.append([m if row is None else m - row for row in rows])
        for e in range(2):
            cs = slice(e * blk, (e + 1) * blk)
            for j in range(qi + 1):
                ks = slice(j * blk, (j + 1) * blk)
                p_ref[ks, cs] = jnp.exp2(s_ref[ks, cs] - shifts[e][j]).astype(BF16)
        for e in range(2):
            cs = slice(e * blk, (e + 1) * blk)
            ot = jnp.dot(vte_ref[e, :, 0:keys], p_ref[0:keys, cs], preferred_element_type=F32)
            denom = ot[(1 - e) * HEAD_DIM:(1 - e) * HEAD_DIM + 1, :]
            ot_ref[e * HEAD_DIM:(e + 1) * HEAD_DIM, qs] = ot[e * HEAD_DIM:(e + 1) * HEAD_DIM, :] / denom
    o_ref[...] = ot_ref[...].T


def _moba_prompt(rel_bias, qt3, kt3, vt3):
    slab = pl.BlockSpec((None, PAIR, SEQ), lambda b, hp: (b, hp, 0))
    return pl.pallas_call(
        _moba_kernel,
        grid=(BATCH, N_HEADS // 2),
        in_specs=[pl.BlockSpec(memory_space=pltpu.SMEM), slab, slab, slab],
        out_specs=pl.BlockSpec((None, SEQ, PAIR), lambda b, hp: (b, 0, hp)),
        out_shape=jax.ShapeDtypeStruct((BATCH, SEQ, D_ATTN), F32),
        scratch_shapes=[
            pltpu.VMEM((SEQ, PAIR), BF16),
            pltpu.VMEM((2, PAIR, SEQ), BF16),
            pltpu.VMEM((2, N_PROMPT_BLOCKS, SEQ), F32),
            pltpu.VMEM((N_HEADS, 2, MOBA_BLOCK, MOBA_BLOCK), F32),
            pltpu.VMEM((SEQ, 2 * MOBA_BLOCK), F32),
            pltpu.VMEM((SEQ, 2 * MOBA_BLOCK), BF16),
            pltpu.VMEM((PAIR, SEQ), F32),
        ],
        compiler_params=pltpu.CompilerParams(dimension_semantics=("arbitrary", "arbitrary"),
                                             vmem_limit_bytes=VMEM_LIMIT),
        name="moba_prompt",
    )(rel_bias, qt3, kt3, vt3)


def _column(mat, n):
    lane = lax.broadcasted_iota(jnp.int32, mat.shape, 1)
    return jnp.sum(jnp.where(lane == n, mat, 0.0), axis=1, keepdims=True)


def _scan_pages(page, n_pages, qb_ref, store_row, g, first_block):
    lane = lax.broadcasted_iota(jnp.int32, (N_HEADS, LANES), 1)
    sub = lax.broadcasted_iota(jnp.int32, (N_HEADS, LANES), 0)
    for h in range(N_HEADS):
        qh = qb_ref[h * HEAD_DIM:(h + 1) * HEAD_DIM, :]
        for bi in range(n_pages // PAGES_PER_BLOCK):
            block_row = None
            for half in range(PAGES_PER_BLOCK):
                i = bi * PAGES_PER_BLOCK + half
                row = jnp.sum(page(i)[h] * qh, axis=0, keepdims=True)
                store_row(h, i, row)
                block_row = row if block_row is None else block_row + row
            gate_hb = jnp.sum(block_row, axis=1, keepdims=True) * (1.0 / MOBA_BLOCK)
            g = jnp.where((sub == h) & (lane == first_block + bi), gate_hb, g)
    return g


def _top_blocks(g):
    lane = lax.broadcasted_iota(jnp.int32, (N_HEADS, LANES), 1)
    lane_f = lane.astype(F32)
    out = jnp.zeros((N_HEADS, LANES), jnp.int32)
    for r in range(MOBA_TOPK):
        best = jnp.max(g, axis=1, keepdims=True)
        pick = jnp.min(jnp.where(g == best, lane_f, float(LANES)), axis=1, keepdims=True)
        out = jnp.where(lane == r, pick.astype(jnp.int32), out)
        g = jnp.where(lane_f == pick, -jnp.inf, g)
    return out


SLABS = MOBA_TOPK * PAGES_PER_BLOCK
assert SLABS + 1 <= SUBLANES


def _sattn_copies(sel_ref, pt_ref, cv_hbm, vbuf, sem, n, slot):
    out = []
    for h in range(N_HEADS):
        for t in range(MOBA_TOPK):
            blk = sel_ref[(n * N_HEADS + h) * MOBA_TOPK + t]
            for half in range(PAGES_PER_BLOCK):
                page = pt_ref[n * N_PAGES + blk * PAGES_PER_BLOCK + half]
                out.append(pltpu.make_async_copy(cv_hbm.at[page, h], vbuf.at[slot, h, t * PAGES_PER_BLOCK + half],
                                                 sem.at[slot]))
    return out


def _sattn_kernel(sel_ref, pt_ref, rb_ref, lga_ref, lgb_ref, qt_ref, kt_ref, vt_ref, cv_hbm, o_ref,
                  vbuf, tab_ref, self_ref, acc_ref, w_ref, sem):
    n = pl.program_id(0)
    slot = lax.rem(n, 2)
    copies = functools.partial(_sattn_copies, sel_ref, pt_ref, cv_hbm, vbuf, sem)
    scale = HEAD_DIM ** -0.5
    lane = lax.broadcasted_iota(jnp.int32, (SUBLANES, LANES), 1)
    sub = lax.broadcasted_iota(jnp.int32, (SUBLANES, LANES), 0)
    from_inproj = n < INPROJ_SEQS

    @pl.when(n == 0)
    def _():
        for c in copies(0, 0):
            c.start()
        dist = MOBA_BLOCK - lax.broadcasted_iota(jnp.int32, (1, MOBA_BLOCK), 1)
        prod = qt_ref[...] * kt_ref[...]
        for h in range(N_HEADS):
            tab_ref[h:h + 1, :] = _bias_of_distance(dist, rb_ref, h)
            self_ref[h:h + 1, :] = (jnp.sum(prod[h * HEAD_DIM:(h + 1) * HEAD_DIM, :], axis=0, keepdims=True) * scale
                                    + rb_ref[0, h])
        w_ref[...] = jnp.zeros((N_HEADS, LANES), F32)
        acc_ref[...] = jnp.zeros((D_ATTN, DEC_BATCH), F32)

    @pl.when(n + 1 < DEC_BATCH)
    def _():
        for c in copies(n + 1, 1 - slot):
            c.start()

    for c in copies(n, slot):
        c.wait()

    self_col = _column(self_ref[...], n)
    w_tile = w_ref[...]
    lane_wide = lax.broadcasted_iota(jnp.int32, (HEAD_DIM, LANES), 1)
    for h in range(N_HEADS):
        far_bias = rb_ref[N_BUCKETS - 1, h]
        s = jnp.full((SUBLANES, LANES), NEG, F32)
        for t in range(MOBA_TOPK):
            blk = sel_ref[(n * N_HEADS + h) * MOBA_TOPK + t]
            for half in range(PAGES_PER_BLOCK):
                r = t * PAGES_PER_BLOCK + half
                pg = pl.ds(blk * PAGES_PER_BLOCK + half, 1)
                row = jnp.where(from_inproj, lga_ref[h, pg, :], lgb_ref[h, pg, :]) * scale
                bias = jnp.where(blk == N_PAST_BLOCKS - 1, tab_ref[h:h + 1, half * PAGE_SIZE:(half + 1) * PAGE_SIZE],
                                 far_bias)
                s = jnp.where(sub == r, row + bias, s)
        s_new = self_col[h:h + 1, :]
        m = jnp.maximum(jnp.max(jnp.max(s, axis=1, keepdims=True), axis=0, keepdims=True), s_new)
        p = jnp.exp(s - m)
        p_new = jnp.exp(s_new - m)
        l = jnp.sum(jnp.sum(p, axis=1, keepdims=True), axis=0, keepdims=True) + p_new
        acc = jnp.zeros((HEAD_DIM, LANES), F32)
        for r in range(SLABS):
            acc = acc + vbuf[slot, h, r] * p[r:r + 1, :]
        col = jnp.sum(acc, axis=1, keepdims=True) / l
        rows = slice(h * HEAD_DIM, (h + 1) * HEAD_DIM)
        acc_ref[rows, :] = jnp.where(lane_wide == n, col, acc_ref[rows, :])
        w_tile = jnp.where((sub == h) & (lane == n), p_new / l, w_tile)
    w_ref[...] = w_tile

    @pl.when(n == DEC_BATCH - 1)
    def _():
        for h in range(N_HEADS):
            rows = slice(h * HEAD_DIM, (h + 1) * HEAD_DIM)
            o_ref[rows, :] = acc_ref[rows, :] + w_tile[h:h + 1, :] * vt_ref[rows, :]


def _attn_sample(sel_flat, pt_flat, rel_bias, logits_a, logits_b, qst, kst, vst, cache_vt):
    full = pl.BlockSpec((D_ATTN, DEC_BATCH), lambda n, sel, pt: (0, 0))
    seq_logits = (None, N_HEADS, N_PAGES, PAGE_SIZE)
    n_a = logits_a.shape[0]
    assert n_a == INPROJ_SEQS
    grid_spec = pltpu.PrefetchScalarGridSpec(
        num_scalar_prefetch=2,
        grid=(DEC_BATCH,),
        in_specs=[
            pl.BlockSpec(memory_space=pltpu.SMEM),
            pl.BlockSpec(seq_logits, lambda n, sel, pt: (jnp.minimum(n, n_a - 1), 0, 0, 0)),
            pl.BlockSpec(seq_logits, lambda n, sel, pt: (jnp.maximum(n - n_a, 0), 0, 0, 0)),
            full, full, full,
            pl.BlockSpec(memory_space=pl.ANY),
        ],
        out_specs=full,
        scratch_shapes=[
            pltpu.VMEM((2, N_HEADS, SLABS, HEAD_DIM, PAGE_SIZE), F32),
            pltpu.VMEM((N_HEADS, MOBA_BLOCK), F32),
            pltpu.VMEM((N_HEADS, DEC_BATCH), F32),
            pltpu.VMEM((D_ATTN, DEC_BATCH), F32),
            pltpu.VMEM((N_HEADS, DEC_BATCH), F32),
            pltpu.SemaphoreType.DMA((2,)),
        ],
    )
    return pl.pallas_call(
        _sattn_kernel,
        grid_spec=grid_spec,
        out_shape=jax.ShapeDtypeStruct((D_ATTN, DEC_BATCH), F32),
        compiler_params=pltpu.CompilerParams(dimension_semantics=("arbitrary",), vmem_limit_bytes=VMEM_LIMIT),
        name="attn_sample",
    )(sel_flat, pt_flat, rel_bias, logits_a, logits_b, qst, kst, vst, cache_vt)


FF_CHUNK = 1024
N_FF_CHUNKS = D_FF // FF_CHUNK


def _ffn_head(x_ref, attn_ref, conv_ref, gt1_ref, sh2_ref, sc2_ref, ga_ref, gffn_ref, wo_ref):
    attn_n = _rms(attn_ref[...], ga_ref[...]).astype(BF16)
    mixed = (jnp.dot(attn_n, wo_ref[:D_ATTN, :], preferred_element_type=F32)
             + jnp.dot(conv_ref[...], wo_ref[D_ATTN:, :], preferred_element_type=F32))
    x1 = x_ref[...] + gt1_ref[...] * mixed
    h2 = (_rms(x1, gffn_ref[...]) * (1.0 + sc2_ref[...]) + sh2_ref[...]).astype(BF16)
    return x1, h2


def _ffn_chunk(h2, w1_ref, w2_ref, c):
    f = jnp.dot(h2, w1_ref[:, c * FF_CHUNK:(c + 1) * FF_CHUNK], preferred_element_type=F32)
    f = jnp.square(jnp.maximum(f, 0.0)).astype(BF16)
    return jnp.dot(f, w2_ref[c * FF_CHUNK:(c + 1) * FF_CHUNK, :], preferred_element_type=F32)


def _ffn_kernel(x_ref, attn_ref, conv_ref, gt1_ref, sh2_ref, sc2_ref, gt2_ref, ga_ref, gffn_ref, gfin_ref,
                wo_ref, w1_ref, w2_ref, y_ref):
    x1, h2 = _ffn_head(x_ref, attn_ref, conv_ref, gt1_ref, sh2_ref, sc2_ref, ga_ref, gffn_ref, wo_ref)
    acc = _ffn_chunk(h2, w1_ref, w2_ref, 0)
    for c in range(1, N_FF_CHUNKS):
        acc = acc + _ffn_chunk(h2, w1_ref, w2_ref, c)
    y_ref[...] = _rms(x1 + gt2_ref[...] * acc, gfin_ref[...])


def _ffn_in_specs(mod_spec, tm):
    row_spec = lambda width: pl.BlockSpec((tm, width), lambda i, *_: (i, 0))
    const = lambda shape: pl.BlockSpec(shape, lambda i, *_: (0, 0), pipeline_mode=pl.Buffered(1))
    return [
        row_spec(D_MODEL), row_spec(D_ATTN), row_spec(D_CONV),
        mod_spec(2), mod_spec(3), mod_spec(4), mod_spec(5),
        const((1, D_ATTN)), const((1, D_MODEL)), const((1, D_MODEL)),
        const((D_MODEL, D_MODEL)), const((D_MODEL, D_FF)), const((D_FF, D_MODEL)),
    ]


def _ffn(x2d, attn2d, conv2d, mod, mod_spec, g_attn, g_ffn, g_final, wo_b, w1_b, w2_b, tm):
    rows = x2d.shape[0]
    return pl.pallas_call(
        _ffn_kernel,
        grid=(rows // tm,),
        in_specs=_ffn_in_specs(mod_spec, tm),
        out_specs=pl.BlockSpec((tm, D_MODEL), lambda i: (i, 0)),
        out_shape=jax.ShapeDtypeStruct((rows, D_MODEL), F32),
        compiler_params=pltpu.CompilerParams(dimension_semantics=("arbitrary",), vmem_limit_bytes=VMEM_LIMIT),
        name="ffn",
    )(x2d, attn2d, conv2d, mod, mod, mod, mod, g_attn, g_ffn, g_final, wo_b, w1_b, w2_b)


SCAN_REGION_PAGES = 32
REGIONS_PER_SEQ = N_PAGES // SCAN_REGION_PAGES
REGION_BLOCKS = SCAN_REGION_PAGES // PAGES_PER_BLOCK
INPROJ_TM = 512
INPROJ_SCAN_SEQS = 2
FFN_SCAN_TM = 256
FFN_SCAN_SEQS = 1
INPROJ_STEPS = BATCH * SEQ // INPROJ_TM
FFN_STEPS = BATCH * SEQ // FFN_SCAN_TM
INPROJ_SEQS = INPROJ_STEPS * INPROJ_SCAN_SEQS
assert INPROJ_SEQS + FFN_STEPS * FFN_SCAN_SEQS == DEC_BATCH


class _ScanHost:
    def __init__(self, seqs, n_steps, seq_base):
        self.regions = seqs * REGIONS_PER_SEQ
        self.seqs = seqs
        self.n_steps = n_steps
        self.seq_base = seq_base
        assert self.regions % 2 == 0

    def bind(self, pt_ref, qst_ref, ck_hbm, lg_ref, idx_ref, kbuf, qb_ref, sem):
        self.pt_ref, self.qst_ref, self.ck_hbm = pt_ref, qst_ref, ck_hbm
        self.lg_ref, self.idx_ref, self.kbuf, self.qb_ref, self.sem = lg_ref, idx_ref, kbuf, qb_ref, sem
        self.step = pl.program_id(0)
        self.gate = None

    def copies(self, region, slot):
        first = self.seq_base * N_PAGES + region * SCAN_REGION_PAGES
        return [pltpu.make_async_copy(self.ck_hbm.at[self.pt_ref[first + i]], self.kbuf.at[slot, i], self.sem.at[slot])
                for i in range(SCAN_REGION_PAGES)]

    def prologue(self):
        @pl.when(self.step == 0)
        def _():
            for c in self.copies(0, 0):
                c.start()

    def region(self, r):
        region = self.step * self.regions + r
        last_region = self.n_steps * self.regions - 1
        slot = r % 2
        for c in self.copies(jnp.minimum(region + 1, last_region), 1 - slot):
            c.start()
        for c in self.copies(region, slot):
            c.wait()
        seq_local, part = divmod(r, REGIONS_PER_SEQ)
        if part == 0:
            n = self.seq_base + self.step * self.seqs + seq_local
            self.qb_ref[...] = jnp.broadcast_to(_column(self.qst_ref[...], n), (D_ATTN, LANES))
            self.gate = jnp.full((N_HEADS, LANES), -jnp.inf, F32)

        def store_row(h, i, row):
            pg = part * SCAN_REGION_PAGES + i
            self.lg_ref[seq_local, h, pg:pg + 1, :] = row

        self.gate = _scan_pages(lambda i: self.kbuf.at[slot, i], SCAN_REGION_PAGES, self.qb_ref, store_row, self.gate,
                                part * REGION_BLOCKS)
        if part == REGIONS_PER_SEQ - 1:
            self.idx_ref[seq_local] = _top_blocks(self.gate)

    def epilogue(self):
        @pl.when(self.step == self.n_steps - 1)
        def _():
            for c in self.copies(self.n_steps * self.regions - 1, 0):
                c.wait()

    def specs(self):
        in_specs = [pl.BlockSpec((D_ATTN, DEC_BATCH), lambda i, pt: (0, 0)), pl.BlockSpec(memory_space=pl.ANY)]
        out_specs = [pl.BlockSpec((self.seqs, N_HEADS, N_PAGES, PAGE_SIZE), lambda i, pt: (i, 0, 0, 0)),
                     pl.BlockSpec((self.seqs, N_HEADS, LANES), lambda i, pt: (i, 0, 0))]
        n_seqs = self.n_steps * self.seqs
        out_shapes = [jax.ShapeDtypeStruct((n_seqs, N_HEADS, N_PAGES, PAGE_SIZE), F32),
                      jax.ShapeDtypeStruct((n_seqs, N_HEADS, LANES), jnp.int32)]
        scratch = [pltpu.VMEM((2, SCAN_REGION_PAGES, N_HEADS, HEAD_DIM, PAGE_SIZE), F32),
                   pltpu.VMEM((D_ATTN, LANES), F32),
                   pltpu.SemaphoreType.DMA((2,))]
        return in_specs, out_specs, out_shapes, scratch


FFN_HOST = (FFN_SCAN_SEQS, FFN_STEPS, INPROJ_SEQS)
INPROJ_HOST = (INPROJ_SCAN_SEQS, INPROJ_STEPS, 0)


def _ffn_scan_kernel(pt_ref, x_ref, attn_ref, conv_ref, gt1_ref, sh2_ref, sc2_ref, gt2_ref, ga_ref, gffn_ref,
                     gfin_ref, wo_ref, w1_ref, w2_ref, qst_ref, ck_hbm, y_ref, lg_ref, idx_ref, kbuf, qb_ref, sem):
    scan = _ScanHost(*FFN_HOST)
    scan.bind(pt_ref, qst_ref, ck_hbm, lg_ref, idx_ref, kbuf, qb_ref, sem)
    scan.prologue()
    chunks_per_region = N_FF_CHUNKS // scan.regions
    x1 = h2 = acc = None
    for r in range(scan.regions):
        scan.region(r)
        if r == 0:
            x1, h2 = _ffn_head(x_ref, attn_ref, conv_ref, gt1_ref, sh2_ref, sc2_ref, ga_ref, gffn_ref, wo_ref)
        for c in range(r * chunks_per_region, (r + 1) * chunks_per_region):
            part = _ffn_chunk(h2, w1_ref, w2_ref, c)
            acc = part if acc is None else acc + part
    y_ref[...] = _rms(x1 + gt2_ref[...] * acc, gfin_ref[...])
    scan.epilogue()


def _ffn_scan(pt_flat, x2d, attn2d, conv2d, mod, mod_spec, g_attn, g_ffn, g_final, wo_b, w1_b, w2_b, qst, cache_kt):
    scan_in, scan_out, scan_shapes, scan_scratch = _ScanHost(*FFN_HOST).specs()
    grid_spec = pltpu.PrefetchScalarGridSpec(
        num_scalar_prefetch=1,
        grid=(FFN_STEPS,),
        in_specs=_ffn_in_specs(mod_spec, FFN_SCAN_TM) + scan_in,
        out_specs=[pl.BlockSpec((FFN_SCAN_TM, D_MODEL), lambda i, pt: (i, 0))] + scan_out,
        scratch_shapes=scan_scratch,
    )
    return pl.pallas_call(
        _ffn_scan_kernel,
        grid_spec=grid_spec,
        out_shape=[jax.ShapeDtypeStruct((BATCH * SEQ, D_MODEL), F32)] + scan_shapes,
        compiler_params=pltpu.CompilerParams(dimension_semantics=("arbitrary",), vmem_limit_bytes=VMEM_LIMIT),
        name="ffn_scan",
    )(pt_flat, x2d, attn2d, conv2d, mod, mod, mod, mod, g_attn, g_ffn, g_final, wo_b, w1_b, w2_b, qst, cache_kt)


INPROJ_TILES_PER_SEQ = SEQ // INPROJ_TM
assert INPROJ_TM % CONV_TM == 0 and INPROJ_TM // CONV_TM + 2 == INPROJ_SCAN_SEQS * REGIONS_PER_SEQ


def _inproj_scan_kernel(pt_ref, x_ref, sh_ref, sc_ref, g_ref, w_ref, wdw_ref, bdw_ref, lng_ref, lnb_ref, gco_ref,
                        qst_ref, ck_hbm, qt_ref, kt_ref, vt_ref, conv_ref, hist_ref, lg_ref, idx_ref,
                        ubuf, shift_ref, kbuf, qb_ref, sem):
    tm = INPROJ_TM
    scan = _ScanHost(*INPROJ_HOST)
    scan.bind(pt_ref, qst_ref, ck_hbm, lg_ref, idx_ref, kbuf, qb_ref, sem)
    scan.prologue()
    step = pl.program_id(0)

    @pl.when(step == 0)
    def _():
        ubuf[...] = jnp.zeros((CONV_HALO + tm, D_CONV), F32)

    hb = None

    def proj(i):
        return jnp.dot(hb, w_ref[:, i * D_ATTN:(i + 1) * D_ATTN], preferred_element_type=F32)

    for r in range(scan.regions):
        scan.region(r)
        if r == 0:
            hb = (_rms(x_ref[...], g_ref[...]) * (1.0 + sc_ref[...]) + sh_ref[...]).astype(BF16)
            u = proj(3) * jax.nn.sigmoid(proj(4))
            first_tile = lax.rem(step, INPROJ_TILES_PER_SEQ) == 0
            ubuf[0:CONV_HALO, :] = jnp.where(first_tile, 0.0, ubuf[tm:tm + CONV_HALO, :])
            ubuf[CONV_HALO:, :] = u
            hist_ref[...] = u[tm - CONV_HALO:, :]
        elif r < scan.regions - 1:
            out = qt_ref if r == 1 else kt_ref
            out[...] = proj(r - 1).T
            t0 = (r - 1) * CONV_TM
            conv = _conv_window(ubuf[t0:t0 + CONV_HALO + CONV_TM, :], wdw_ref, bdw_ref, lng_ref, lnb_ref, gco_ref,
                                shift_ref)
            conv_ref[t0:t0 + CONV_TM, :] = conv.astype(BF16)
        else:
            vt_ref[...] = proj(2).T
    scan.epilogue()


def _inproj_scan(pt_flat, x2d, mod, mod_spec, g_mix, w_in_b, w_dw, b_dw, ln_g, ln_b, g_co, qst, cache_kt):
    tm = INPROJ_TM
    scan_in, scan_out, scan_shapes, scan_scratch = _ScanHost(*INPROJ_HOST).specs()
    t_spec = pl.BlockSpec((None, D_ATTN, tm),
                          lambda i, pt: (i // INPROJ_TILES_PER_SEQ, 0, i % INPROJ_TILES_PER_SEQ))
    t_out = jax.ShapeDtypeStruct((BATCH, D_ATTN, SEQ), F32)
    vec = pl.BlockSpec((1, D_CONV), lambda i, pt: (0, 0))
    grid_spec = pltpu.PrefetchScalarGridSpec(
        num_scalar_prefetch=1,
        grid=(INPROJ_STEPS,),
        in_specs=[
            pl.BlockSpec((tm, D_MODEL), lambda i, pt: (i, 0)),
            mod_spec(0),
            mod_spec(1),
            pl.BlockSpec((1, D_MODEL), lambda i, pt: (0, 0)),
            pl.BlockSpec((D_MODEL, D_IN), lambda i, pt: (0, 0), pipeline_mode=pl.Buffered(1)),
            pl.BlockSpec((CONV_WIDTH, D_CONV), lambda i, pt: (0, 0)),
            vec, vec, vec, vec,
        ] + scan_in,
        out_specs=[
            t_spec, t_spec, t_spec,
            pl.BlockSpec((tm, D_CONV), lambda i, pt: (i, 0)),
            pl.BlockSpec((None, CONV_HALO, D_CONV), lambda i, pt: (i // INPROJ_TILES_PER_SEQ, 0, 0)),
        ] + scan_out,
        scratch_shapes=[
            pltpu.VMEM((CONV_HALO + tm, D_CONV), F32),
            pltpu.VMEM((SUBLANES - 1, CONV_SPAN, D_CONV), F32),
        ] + scan_scratch,
    )
    return pl.pallas_call(
        _inproj_scan_kernel,
        grid_spec=grid_spec,
        out_shape=[t_out, t_out, t_out,
                   jax.ShapeDtypeStruct((BATCH * SEQ, D_CONV), BF16),
                   jax.ShapeDtypeStruct((BATCH, CONV_HALO, D_CONV), F32)] + scan_shapes,
        compiler_params=pltpu.CompilerParams(dimension_semantics=("arbitrary",), vmem_limit_bytes=VMEM_LIMIT),
        name="inproj_scan",
    )(pt_flat, x2d, mod, mod, g_mix, w_in_b, w_dw, b_dw, ln_g, ln_b, g_co, qst, cache_kt)


def kernel(x_prompt, x_sample, c_prompt, c_sample, cache_k, cache_v, state_conv, page_table, rel_bias, w_ada, b_ada, g_mix, w_in, w_dw, b_dw, ln_conv_g, ln_conv_b, g_attn_out, g_conv_out, w_out, g_ffn, w_ff1, w_ff2, g_final):
    w_in_b = w_in[0].astype(BF16)
    wo_b = w_out[0].astype(BF16)
    w1_b = w_ff1[0].astype(BF16)
    w2_b = w_ff2[0].astype(BF16)
    g_fin = g_final.reshape(1, D_MODEL)

    mod = _mod(jnp.concatenate([c_prompt, c_sample], axis=0), w_ada[0], b_ada)
    mod_p = mod[:BATCH].reshape(BATCH, 6, 1, D_MODEL)
    mod_s = mod[BATCH:]

    xs = x_sample.reshape(DEC_BATCH, D_MODEL)
    qst, kst, vst, us = _inproj(xs, mod_s, _mod_specs_sample, g_mix, w_in_b, DEC_BATCH, (D_ATTN, DEC_BATCH),
                                pl.BlockSpec((D_ATTN, DEC_BATCH), lambda i: (0, 0)))
    cache_kt = jnp.transpose(cache_k[0], (0, 2, 3, 1))
    cache_vt = jnp.transpose(cache_v[0], (0, 2, 3, 1))
    pt_flat = page_table.reshape(-1)

    xp = x_prompt.reshape(BATCH * SEQ, D_MODEL)
    qt, kt, vt, conv_n, u_tail, logits_a, sel_a = _inproj_scan(
        pt_flat, xp, mod_p, functools.partial(_mod_specs_prompt, INPROJ_TM), g_mix, w_in_b,
        w_dw[0], b_dw, ln_conv_g, ln_conv_b, g_conv_out, qst, cache_kt)
    attn = _moba_prompt(rel_bias, qt, kt, vt)
    y_p, logits_b, sel_b = _ffn_scan(
        pt_flat, xp, attn.reshape(BATCH * SEQ, D_ATTN), conv_n, mod_p,
        functools.partial(_mod_specs_prompt, FFN_SCAN_TM), g_attn_out, g_ffn, g_fin, wo_b, w1_b, w2_b, qst, cache_kt)

    hist_t = jnp.transpose(state_conv[0], (1, 0, 2))
    conv_s, new_hist_t = _conv_sample(hist_t, us, w_dw[0], b_dw, ln_conv_g, ln_conv_b, g_conv_out)
    sel_flat = jnp.concatenate([sel_a, sel_b], axis=0)[:, :, :MOBA_TOPK].reshape(-1)
    attn_st = _attn_sample(sel_flat, pt_flat, rel_bias, logits_a, logits_b, qst, kst, vst, cache_vt)
    y_s = _ffn(xs, attn_st.T, conv_s, mod_s, _mod_specs_sample,
               g_attn_out, g_ffn, g_fin, wo_b, w1_b, w2_b, DEC_BATCH)

    kv_p = lambda a: jnp.transpose(a.reshape(1, BATCH, N_HEADS, HEAD_DIM, SEQ), (0, 1, 4, 2, 3))
    kv_s = lambda a: jnp.transpose(a.reshape(1, 1, N_HEADS, HEAD_DIM, DEC_BATCH), (0, 4, 1, 2, 3))
    hist_p = u_tail[:, CONV_HALO - N_HIST:, :][None]
    hist_s = jnp.transpose(new_hist_t, (1, 0, 2))[None]
    return (y_p.reshape(BATCH, SEQ, D_MODEL), y_s.reshape(DEC_BATCH, 1, D_MODEL),
            kv_p(kt), kv_p(vt), hist_p, kv_s(kst), kv_s(vst), hist_s)
```

```python
import functools
import math

import numpy as np
import jax
import jax.numpy as jnp
from jax import lax
from jax.experimental import pallas as pl
from jax.experimental.pallas import tpu as pltpu

D_MODEL = 1024
BATCH = 8
SEQ = 2048
DEC_BATCH = 128
PAST_LEN = 8192
PAGE_SIZE = 128
D_ATTN = 512
D_CONV = 512
HEAD_DIM = 64
N_HEADS = 8
CONV_WIDTH = 31
MOBA_BLOCK = 256
MOBA_TOPK = 3
N_BUCKETS = 32
MAX_DISTANCE = 128
D_FF = 4096
EPS = 1e-6
D_IN = 3 * D_ATTN + 2 * D_CONV
N_PAGES = PAST_LEN // PAGE_SIZE
N_PAST_BLOCKS = PAST_LEN // MOBA_BLOCK
PAGES_PER_BLOCK = MOBA_BLOCK // PAGE_SIZE
N_PROMPT_BLOCKS = SEQ // MOBA_BLOCK
LANES = 128
SUBLANES = 8

F32 = jnp.float32
BF16 = jnp.bfloat16
NEG = -1e30
LOG2E = math.log2(math.e)
VMEM_LIMIT = 56 * 1024 * 1024


def _bucket_thresholds():
    n = np.arange(0, 4 * MAX_DISTANCE)
    max_exact = N_BUCKETS // 2
    ratio = np.maximum(n, max_exact).astype(np.float32) / np.float32(max_exact)
    val = np.log(ratio) / np.float32(math.log(MAX_DISTANCE / max_exact)) * np.float32(N_BUCKETS - max_exact)
    large = np.minimum(max_exact + val.astype(np.int32), N_BUCKETS - 1)
    bucket = np.where(n < max_exact, n, large)
    assert np.all(np.diff(bucket) >= 0) and bucket[-1] == N_BUCKETS - 1
    return [int(np.argmax(bucket >= b)) for b in range(N_BUCKETS)]


BUCKET_START = _bucket_thresholds()
FAR_DISTANCE = BUCKET_START[N_BUCKETS - 1]
assert FAR_DISTANCE <= MOBA_BLOCK


def _rms(x, g):
    return x * lax.rsqrt(jnp.mean(x * x, axis=-1, keepdims=True) + EPS) * g


def _bias_of_distance(dist, rb_ref, h):
    val = jnp.full(dist.shape, rb_ref[0, h], F32)
    for b in range(1, N_BUCKETS):
        val = jnp.where(dist >= BUCKET_START[b], rb_ref[b, h], val)
    return val


def _mod_kernel(c_ref, w_ref, b_ref, o_ref):
    c = c_ref[...]
    s = (c * jax.nn.sigmoid(c)).astype(BF16)
    o_ref[...] = jnp.dot(s, w_ref[...].astype(BF16), preferred_element_type=F32) + b_ref[...]


def _mod(c_all, w_ada, b_ada):
    rows = c_all.shape[0]
    bn = 1024
    return pl.pallas_call(
        _mod_kernel,
        grid=(6 * D_MODEL // bn,),
        in_specs=[
            pl.BlockSpec((rows, D_MODEL), lambda j: (0, 0)),
            pl.BlockSpec((D_MODEL, bn), lambda j: (0, j)),
            pl.BlockSpec((1, bn), lambda j: (0, j)),
        ],
        out_specs=pl.BlockSpec((rows, bn), lambda j: (0, j)),
        out_shape=jax.ShapeDtypeStruct((rows, 6 * D_MODEL), F32),
        compiler_params=pltpu.CompilerParams(dimension_semantics=("arbitrary",), vmem_limit_bytes=VMEM_LIMIT),
        name="mod",
    )(c_all, w_ada, b_ada)


def _mod_specs_prompt(tm, k):
    tiles_per_seq = SEQ // tm
    return pl.BlockSpec((None, None, 1, D_MODEL), lambda i, *_: (i // tiles_per_seq, k, 0, 0))


def _mod_specs_sample(k):
    return pl.BlockSpec((DEC_BATCH, D_MODEL), lambda i, *_: (0, k))


def _inproj_kernel(x_ref, sh_ref, sc_ref, g_ref, w_ref, qt_ref, kt_ref, vt_ref, u_ref):
    h = _rms(x_ref[...], g_ref[...]) * (1.0 + sc_ref[...]) + sh_ref[...]
    hb = h.astype(BF16)

    def proj(i):
        return jnp.dot(hb, w_ref[:, i * D_ATTN:(i + 1) * D_ATTN], preferred_element_type=F32)

    qt_ref[...] = proj(0).T
    kt_ref[...] = proj(1).T
    vt_ref[...] = proj(2).T
    a = proj(3)
    g = proj(4)
    u_ref[...] = a * jax.nn.sigmoid(g)


def _inproj(x2d, mod, mod_spec, g_mix, w_in_b, tm, t_shape, t_spec):
    rows = x2d.shape[0]
    row_spec = lambda width: pl.BlockSpec((tm, width), lambda i: (i, 0))
    t_out = jax.ShapeDtypeStruct(t_shape, F32)
    return pl.pallas_call(
        _inproj_kernel,
        grid=(rows // tm,),
        in_specs=[
            row_spec(D_MODEL),
            mod_spec(0),
            mod_spec(1),
            pl.BlockSpec((1, D_MODEL), lambda i: (0, 0)),
            pl.BlockSpec((D_MODEL, D_IN), lambda i: (0, 0)),
        ],
        out_specs=[t_spec, t_spec, t_spec, row_spec(D_CONV)],
        out_shape=[t_out, t_out, t_out, jax.ShapeDtypeStruct((rows, D_CONV), F32)],
        compiler_params=pltpu.CompilerParams(dimension_semantics=("arbitrary",), vmem_limit_bytes=VMEM_LIMIT),
        name="inproj",
    )(x2d, mod, mod, g_mix, w_in_b)


CONV_TM = 128
CONV_HALO = 32


def _conv_post(y, lng, lnb, gco):
    mu = jnp.mean(y, axis=-1, keepdims=True)
    yc = y - mu
    yn = yc * lax.rsqrt(jnp.mean(yc * yc, axis=-1, keepdims=True) + EPS) * lng + lnb
    s = yn * jax.nn.sigmoid(yn)
    return _rms(s, gco)


CONV_SPAN = CONV_HALO + CONV_TM - SUBLANES


def _conv_window(win, wdw_ref, bdw_ref, lng_ref, lnb_ref, gco_ref, sh_ref):
    first = CONV_HALO - (CONV_WIDTH - 1)
    acc = jnp.zeros((CONV_TM, D_CONV), F32)
    for r in range(SUBLANES):
        offs = [o for o in range(first, first + CONV_WIDTH) if o % SUBLANES == r]
        if r > 0:
            sh_ref[r - 1] = win[r:r + CONV_SPAN, :]
        for o in offs:
            a8 = o - r
            tap = win[a8:a8 + CONV_TM, :] if r == 0 else sh_ref[r - 1, a8:a8 + CONV_TM, :]
            acc = acc + tap * wdw_ref[o - first:o - first + 1, :]
    y = acc + bdw_ref[...]
    return _conv_post(y, lng_ref[...], lnb_ref[...], gco_ref[...])


SCONV_TN = 32
N_HIST = CONV_WIDTH - 1


def _sconv_kernel(hist_ref, u_ref, wdw_ref, bdw_ref, lng_ref, lnb_ref, gco_ref, o_ref, nh_ref):
    u = u_ref[...]
    y = u * wdw_ref[N_HIST:CONV_WIDTH, :] + bdw_ref[...]
    for w in range(N_HIST):
        y = y + hist_ref[w] * wdw_ref[w:w + 1, :]
    o_ref[...] = _conv_post(y, lng_ref[...], lnb_ref[...], gco_ref[...]).astype(BF16)
    for w in range(N_HIST - 1):
        nh_ref[w] = hist_ref[w + 1]
    nh_ref[N_HIST - 1] = u


def _conv_sample(hist_t, u, w_dw, b_dw, ln_g, ln_b, g_co):
    vec = pl.BlockSpec((1, D_CONV), lambda i: (0, 0))
    hist_spec = pl.BlockSpec((N_HIST, SCONV_TN, D_CONV), lambda i: (0, i, 0))
    return pl.pallas_call(
        _sconv_kernel,
        grid=(DEC_BATCH // SCONV_TN,),
        in_specs=[
            hist_spec,
            pl.BlockSpec((SCONV_TN, D_CONV), lambda i: (i, 0)),
            pl.BlockSpec((CONV_WIDTH, D_CONV), lambda i: (0, 0)),
            vec, vec, vec, vec,
        ],
        out_specs=[pl.BlockSpec((SCONV_TN, D_CONV), lambda i: (i, 0)), hist_spec],
        out_shape=[jax.ShapeDtypeStruct((DEC_BATCH, D_CONV), BF16),
                   jax.ShapeDtypeStruct((N_HIST, DEC_BATCH, D_CONV), F32)],
        compiler_params=pltpu.CompilerParams(dimension_semantics=("arbitrary",), vmem_limit_bytes=VMEM_LIMIT),
        name="conv_sample",
    )(hist_t, u, w_dw, b_dw, ln_g, ln_b, g_co)


PAIR = 2 * HEAD_DIM


def _moba_kernel(rb_ref, qt_ref, kt_ref, vt_ref, o_ref, kb_ref, vte_ref, mask_ref, bias_ref, s_ref, p_ref, ot_ref):
    b = pl.program_id(0)
    hp = pl.program_id(1)
    blk = MOBA_BLOCK
    nb = N_PROMPT_BLOCKS
    drow = lax.broadcasted_iota(jnp.int32, (PAIR, 1), 0)
    head_rows = [(drow >= e * HEAD_DIM) & (drow < (e + 1) * HEAD_DIM) for e in range(2)]

    @pl.when(b == 0)
    def _():
        kk = lax.broadcasted_iota(jnp.int32, (blk, blk), 0)
        qq = lax.broadcasted_iota(jnp.int32, (blk, blk), 1)
        d0 = qq - kk
        for e in range(2):
            h = 2 * hp + e
            bias_ref[h, 0] = jnp.where(d0 >= 0, _bias_of_distance(jnp.maximum(d0, 0), rb_ref, h) * LOG2E, NEG)
            bias_ref[h, 1] = _bias_of_distance(d0 + blk, rb_ref, h) * LOG2E

    km_cols = []
    for j in range(nb):
        ktj = kt_ref[:, j * blk:(j + 1) * blk]
        kb_ref[j * blk:(j + 1) * blk, :] = ktj.T.astype(BF16)
        km_cols.append(jnp.sum(ktj, axis=1, keepdims=True) * (1.0 / blk))
    vt = vt_ref[...]
    for e in range(2):
        vte_ref[e] = jnp.where(head_rows[e], vt, 1.0).astype(BF16)

    qt = qt_ref[...]
    qblk = lax.broadcasted_iota(jnp.int32, (1, SEQ), 1) // blk
    for e in range(2):
        h = 2 * hp + e
        far2 = rb_ref[N_BUCKETS - 1, h] * LOG2E
        qh = qt[e * HEAD_DIM:(e + 1) * HEAD_DIM, :]
        gates = [jnp.sum(qh * km_cols[j][e * HEAD_DIM:(e + 1) * HEAD_DIM, :], axis=0, keepdims=True)
                 for j in range(nb - 1)]
        gate = jnp.concatenate(gates + [jnp.zeros((1, SEQ), F32)], axis=0)
        row = lax.broadcasted_iota(jnp.int32, (nb, SEQ), 0)
        cnt = jnp.zeros((nb, SEQ), jnp.int32)
        for i in range(nb - 1):
            beats = jnp.where(gates[i] > gate, 1, jnp.where((gates[i] == gate) & (i < row), 1, 0))
            cnt = cnt + jnp.where(i < qblk, beats, 0)
        keep = jnp.where(row < qblk, jnp.where(cnt < MOBA_TOPK, 0.0, NEG), NEG)
        mask_ref[e] = keep + far2

    qscale = (HEAD_DIM ** -0.5) * LOG2E
    for qi in range(nb):
        qs = slice(qi * blk, (qi + 1) * blk)
        keys = (qi + 1) * blk
        qtile = qt_ref[:, qs] * qscale
        qcat = jnp.concatenate([jnp.where(head_rows[e], qtile, 0.0) for e in range(2)], axis=1).astype(BF16)
        s_ref[0:keys, :] = jnp.dot(kb_ref[0:keys, :], qcat, preferred_element_type=F32)
        shifts = []
        for e in range(2):
            h = 2 * hp + e
            far2 = rb_ref[N_BUCKETS - 1, h] * LOG2E
            cs = slice(e * blk, (e + 1) * blk)
            rows = []
            tops = []
            for j in range(qi + 1):
                ks = slice(j * blk, (j + 1) * blk)
                if j == qi:
                    sj = s_ref[ks, cs] + bias_ref[h, 0]
                    s_ref[ks, cs] = sj
                    row = None
                elif j == qi - 1:
                    sj = s_ref[ks, cs] + bias_ref[h, 1]
                    s_ref[ks, cs] = sj
                    row = mask_ref[e, j:j + 1, qs] - far2
                else:
                    sj = s_ref[ks, cs]
                    row = mask_ref[e, j:j + 1, qs]
                top = jnp.max(sj, axis=0, keepdims=True)
                rows.append(row)
                tops.append(top if row is None else top + row)
            m = functools.reduce(jnp.maximum, tops)
            shifts.append([m if row is None else m - row for row in rows])
        for e in range(2):
            cs = slice(e * blk, (e + 1) * blk)
            for j in range(qi + 1):
                ks = slice(j * blk, (j + 1) * blk)
                p_ref[ks, cs] = jnp.exp2(s_ref[ks, cs] - shifts[e][j]).astype(BF16)
        for e in range(2):
            cs = slice(e * blk, (e + 1) * blk)
            ot = jnp.dot(vte_ref[e, :, 0:keys], p_ref[0:keys, cs], preferred_element_type=F32)
            denom = ot[(1 - e) * HEAD_DIM:(1 - e) * HEAD_DIM + 1, :]
            ot_ref[e * HEAD_DIM:(e + 1) * HEAD_DIM, qs] = ot[e * HEAD_DIM:(e + 1) * HEAD_DIM, :] / denom
    o_ref[...] = ot_ref[...].T


def _moba_prompt(rel_bias, qt3, kt3, vt3):
    slab = pl.BlockSpec((None, PAIR, SEQ), lambda b, hp: (b, hp, 0))
    return pl.pallas_call(
        _moba_kernel,
        grid=(BATCH, N_HEADS // 2),
        in_specs=[pl.BlockSpec(memory_space=pltpu.SMEM), slab, slab, slab],
        out_specs=pl.BlockSpec((None, SEQ, PAIR), lambda b, hp: (b, 0, hp)),
        out_shape=jax.ShapeDtypeStruct((BATCH, SEQ, D_ATTN), F32),
        scratch_shapes=[
            pltpu.VMEM((SEQ, PAIR), BF16),
            pltpu.VMEM((2, PAIR, SEQ), BF16),
            pltpu.VMEM((2, N_PROMPT_BLOCKS, SEQ), F32),
            pltpu.VMEM((N_HEADS, 2, MOBA_BLOCK, MOBA_BLOCK), F32),
            pltpu.VMEM((SEQ, 2 * MOBA_BLOCK), F32),
            pltpu.VMEM((SEQ, 2 * MOBA_BLOCK), BF16),
            pltpu.VMEM((PAIR, SEQ), F32),
        ],
        compiler_params=pltpu.CompilerParams(dimension_semantics=("arbitrary", "arbitrary"),
                                             vmem_limit_bytes=VMEM_LIMIT),
        name="moba_prompt",
    )(rel_bias, qt3, kt3, vt3)


def _column(mat, n):
    lane = lax.broadcasted_iota(jnp.int32, mat.shape, 1)
    return jnp.sum(jnp.where(lane == n, mat, 0.0), axis=1, keepdims=True)


def _scan_pages(page, n_pages, qb_ref, store_row, g, first_block):
    lane = lax.broadcasted_iota(jnp.int32, (N_HEADS, LANES), 1)
    sub = lax.broadcasted_iota(jnp.int32, (N_HEADS, LANES), 0)
    for h in range(N_HEADS):
        qh = qb_ref[h * HEAD_DIM:(h + 1) * HEAD_DIM, :]
        for bi in range(n_pages // PAGES_PER_BLOCK):
            block_row = None
            for half in range(PAGES_PER_BLOCK):
                i = bi * PAGES_PER_BLOCK + half
                row = jnp.sum(page(i)[h] * qh, axis=0, keepdims=True)
                store_row(h, i, row)
                block_row = row if block_row is None else block_row + row
            gate_hb = jnp.sum(block_row, axis=1, keepdims=True) * (1.0 / MOBA_BLOCK)
            g = jnp.where((sub == h) & (lane == first_block + bi), gate_hb, g)
    return g


def _top_blocks(g):
    lane = lax.broadcasted_iota(jnp.int32, (N_HEADS, LANES), 1)
    lane_f = lane.astype(F32)
    out = jnp.zeros((N_HEADS, LANES), jnp.int32)
    for r in range(MOBA_TOPK):
        best = jnp.max(g, axis=1, keepdims=True)
        pick = jnp.min(jnp.where(g == best, lane_f, float(LANES)), axis=1, keepdims=True)
        out = jnp.where(lane == r, pick.astype(jnp.int32), out)
        g = jnp.where(lane_f == pick, -jnp.inf, g)
    return out


SLABS = MOBA_TOPK * PAGES_PER_BLOCK
assert SLABS + 1 <= SUBLANES


def _sattn_copies(sel_ref, pt_ref, cv_hbm, vbuf, sem, n, slot):
    out = []
    for h in range(N_HEADS):
        for t in range(MOBA_TOPK):
            blk = sel_ref[(n * N_HEADS + h) * MOBA_TOPK + t]
            for half in range(PAGES_PER_BLOCK):
                page = pt_ref[n * N_PAGES + blk * PAGES_PER_BLOCK + half]
                out.append(pltpu.make_async_copy(cv_hbm.at[page, h], vbuf.at[slot, h, t * PAGES_PER_BLOCK + half],
                                                 sem.at[slot]))
    return out


def _sattn_kernel(sel_ref, pt_ref, rb_ref, lga_ref, lgb_ref, qt_ref, kt_ref, vt_ref, cv_hbm, o_ref,
                  vbuf, tab_ref, self_ref, acc_ref, w_ref, sem):
    n = pl.program_id(0)
    slot = lax.rem(n, 2)
    copies = functools.partial(_sattn_copies, sel_ref, pt_ref, cv_hbm, vbuf, sem)
    scale = HEAD_DIM ** -0.5
    lane = lax.broadcasted_iota(jnp.int32, (SUBLANES, LANES), 1)
    sub = lax.broadcasted_iota(jnp.int32, (SUBLANES, LANES), 0)
    from_inproj = n < INPROJ_SEQS

    @pl.when(n == 0)
    def _():
        for c in copies(0, 0):
            c.start()
        dist = MOBA_BLOCK - lax.broadcasted_iota(jnp.int32, (1, MOBA_BLOCK), 1)
        prod = qt_ref[...] * kt_ref[...]
        for h in range(N_HEADS):
            tab_ref[h:h + 1, :] = _bias_of_distance(dist, rb_ref, h)
            self_ref[h:h + 1, :] = (jnp.sum(prod[h * HEAD_DIM:(h + 1) * HEAD_DIM, :], axis=0, keepdims=True) * scale
                                    + rb_ref[0, h])
        w_ref[...] = jnp.zeros((N_HEADS, LANES), F32)
        acc_ref[...] = jnp.zeros((D_ATTN, DEC_BATCH), F32)

    @pl.when(n + 1 < DEC_BATCH)
    def _():
        for c in copies(n + 1, 1 - slot):
            c.start()

    for c in copies(n, slot):
        c.wait()

    self_col = _column(self_ref[...], n)
    w_tile = w_ref[...]
    lane_wide = lax.broadcasted_iota(jnp.int32, (HEAD_DIM, LANES), 1)
    for h in range(N_HEADS):
        far_bias = rb_ref[N_BUCKETS - 1, h]
        s = jnp.full((SUBLANES, LANES), NEG, F32)
        for t in range(MOBA_TOPK):
            blk = sel_ref[(n * N_HEADS + h) * MOBA_TOPK + t]
            for half in range(PAGES_PER_BLOCK):
                r = t * PAGES_PER_BLOCK + half
                pg = pl.ds(blk * PAGES_PER_BLOCK + half, 1)
                row = jnp.where(from_inproj, lga_ref[h, pg, :], lgb_ref[h, pg, :]) * scale
                bias = jnp.where(blk == N_PAST_BLOCKS - 1, tab_ref[h:h + 1, half * PAGE_SIZE:(half + 1) * PAGE_SIZE],
                                 far_bias)
                s = jnp.where(sub == r, row + bias, s)
        s_new = self_col[h:h + 1, :]
        m = jnp.maximum(jnp.max(jnp.max(s, axis=1, keepdims=True), axis=0, keepdims=True), s_new)
        p = jnp.exp(s - m)
        p_new = jnp.exp(s_new - m)
        l = jnp.sum(jnp.sum(p, axis=1, keepdims=True), axis=0, keepdims=True) + p_new
        acc = jnp.zeros((HEAD_DIM, LANES), F32)
        for r in range(SLABS):
            acc = acc + vbuf[slot, h, r] * p[r:r + 1, :]
        col = jnp.sum(acc, axis=1, keepdims=True) / l
        rows = slice(h * HEAD_DIM, (h + 1) * HEAD_DIM)
        acc_ref[rows, :] = jnp.where(lane_wide == n, col, acc_ref[rows, :])
        w_tile = jnp.where((sub == h) & (lane == n), p_new / l, w_tile)
    w_ref[...] = w_tile

    @pl.when(n == DEC_BATCH - 1)
    def _():
        for h in range(N_HEADS):
            rows = slice(h * HEAD_DIM, (h + 1) * HEAD_DIM)
            o_ref[rows, :] = acc_ref[rows, :] + w_tile[h:h + 1, :] * vt_ref[rows, :]


def _attn_sample(sel_flat, pt_flat, rel_bias, logits_a, logits_b, qst, kst, vst, cache_vt):
    full = pl.BlockSpec((D_ATTN, DEC_BATCH), lambda n, sel, pt: (0, 0))
    seq_logits = (None, N_HEADS, N_PAGES, PAGE_SIZE)
    n_a = logits_a.shape[0]
    assert n_a == INPROJ_SEQS
    grid_spec = pltpu.PrefetchScalarGridSpec(
        num_scalar_prefetch=2,
        grid=(DEC_BATCH,),
        in_specs=[
            pl.BlockSpec(memory_space=pltpu.SMEM),
            pl.BlockSpec(seq_logits, lambda n, sel, pt: (jnp.minimum(n, n_a - 1), 0, 0, 0)),
            pl.BlockSpec(seq_logits, lambda n, sel, pt: (jnp.maximum(n - n_a, 0), 0, 0, 0)),
            full, full, full,
            pl.BlockSpec(memory_space=pl.ANY),
        ],
        out_specs=full,
        scratch_shapes=[
            pltpu.VMEM((2, N_HEADS, SLABS, HEAD_DIM, PAGE_SIZE), F32),
            pltpu.VMEM((N_HEADS, MOBA_BLOCK), F32),
            pltpu.VMEM((N_HEADS, DEC_BATCH), F32),
            pltpu.VMEM((D_ATTN, DEC_BATCH), F32),
            pltpu.VMEM((N_HEADS, DEC_BATCH), F32),
            pltpu.SemaphoreType.DMA((2,)),
        ],
    )
    return pl.pallas_call(
        _sattn_kernel,
        grid_spec=grid_spec,
        out_shape=jax.ShapeDtypeStruct((D_ATTN, DEC_BATCH), F32),
        compiler_params=pltpu.CompilerParams(dimension_semantics=("arbitrary",), vmem_limit_bytes=VMEM_LIMIT),
        name="attn_sample",
    )(sel_flat, pt_flat, rel_bias, logits_a, logits_b, qst, kst, vst, cache_vt)


FF_CHUNK = 1024
N_FF_CHUNKS = D_FF // FF_CHUNK


def _ffn_head(x_ref, attn_ref, conv_ref, gt1_ref, sh2_ref, sc2_ref, ga_ref, gffn_ref, wo_ref):
    attn_n = _rms(attn_ref[...], ga_ref[...]).astype(BF16)
    mixed = (jnp.dot(attn_n, wo_ref[:D_ATTN, :], preferred_element_type=F32)
             + jnp.dot(conv_ref[...], wo_ref[D_ATTN:, :], preferred_element_type=F32))
    x1 = x_ref[...] + gt1_ref[...] * mixed
    h2 = (_rms(x1, gffn_ref[...]) * (1.0 + sc2_ref[...]) + sh2_ref[...]).astype(BF16)
    return x1, h2


def _ffn_chunk(h2, w1_ref, w2_ref, c):
    f = jnp.dot(h2, w1_ref[:, c * FF_CHUNK:(c + 1) * FF_CHUNK], preferred_element_type=F32)
    f = jnp.square(jnp.maximum(f, 0.0)).astype(BF16)
    return jnp.dot(f, w2_ref[c * FF_CHUNK:(c + 1) * FF_CHUNK, :], preferred_element_type=F32)


def _ffn_kernel(x_ref, attn_ref, conv_ref, gt1_ref, sh2_ref, sc2_ref, gt2_ref, ga_ref, gffn_ref, gfin_ref,
                wo_ref, w1_ref, w2_ref, y_ref):
    x1, h2 = _ffn_head(x_ref, attn_ref, conv_ref, gt1_ref, sh2_ref, sc2_ref, ga_ref, gffn_ref, wo_ref)
    acc = _ffn_chunk(h2, w1_ref, w2_ref, 0)
    for c in range(1, N_FF_CHUNKS):
        acc = acc + _ffn_chunk(h2, w1_ref, w2_ref, c)
    y_ref[...] = _rms(x1 + gt2_ref[...] * acc, gfin_ref[...])


def _ffn_in_specs(mod_spec, tm):
    row_spec = lambda width: pl.BlockSpec((tm, width), lambda i, *_: (i, 0))
    const = lambda shape: pl.BlockSpec(shape, lambda i, *_: (0, 0), pipeline_mode=pl.Buffered(1))
    return [
        row_spec(D_MODEL), row_spec(D_ATTN), row_spec(D_CONV),
        mod_spec(2), mod_spec(3), mod_spec(4), mod_spec(5),
        const((1, D_ATTN)), const((1, D_MODEL)), const((1, D_MODEL)),
        const((D_MODEL, D_MODEL)), const((D_MODEL, D_FF)), const((D_FF, D_MODEL)),
    ]


def _ffn(x2d, attn2d, conv2d, mod, mod_spec, g_attn, g_ffn, g_final, wo_b, w1_b, w2_b, tm):
    rows = x2d.shape[0]
    return pl.pallas_call(
        _ffn_kernel,
        grid=(rows // tm,),
        in_specs=_ffn_in_specs(mod_spec, tm),
        out_specs=pl.BlockSpec((tm, D_MODEL), lambda i: (i, 0)),
        out_shape=jax.ShapeDtypeStruct((rows, D_MODEL), F32),
        compiler_params=pltpu.CompilerParams(dimension_semantics=("arbitrary",), vmem_limit_bytes=VMEM_LIMIT),
        name="ffn",
    )(x2d, attn2d, conv2d, mod, mod, mod, mod, g_attn, g_ffn, g_final, wo_b, w1_b, w2_b)


SCAN_REGION_PAGES = 16
SCAN_SLOTS = 4
SCAN_AHEAD = SCAN_SLOTS - 1
REGIONS_PER_SEQ = N_PAGES // SCAN_REGION_PAGES
REGION_BLOCKS = SCAN_REGION_PAGES // PAGES_PER_BLOCK
INPROJ_TM = 512
INPROJ_SCAN_SEQS = 2
FFN_SCAN_TM = 256
FFN_SCAN_SEQS = 1
INPROJ_STEPS = BATCH * SEQ // INPROJ_TM
FFN_STEPS = BATCH * SEQ // FFN_SCAN_TM
INPROJ_SEQS = INPROJ_STEPS * INPROJ_SCAN_SEQS
assert INPROJ_SEQS + FFN_STEPS * FFN_SCAN_SEQS == DEC_BATCH


class _ScanHost:
    def __init__(self, seqs, n_steps, seq_base):
        self.regions = seqs * REGIONS_PER_SEQ
        self.seqs = seqs
        self.n_steps = n_steps
        self.seq_base = seq_base
        assert self.regions % SCAN_SLOTS == 0
        assert n_steps * self.regions > SCAN_AHEAD

    def bind(self, pt_ref, qst_ref, ck_hbm, lg_ref, idx_ref, kbuf, qb_ref, sem):
        self.pt_ref, self.qst_ref, self.ck_hbm = pt_ref, qst_ref, ck_hbm
        self.lg_ref, self.idx_ref, self.kbuf, self.qb_ref, self.sem = lg_ref, idx_ref, kbuf, qb_ref, sem
        self.step = pl.program_id(0)
        self.gate = None

    def copies(self, region, slot):
        first = self.seq_base * N_PAGES + region * SCAN_REGION_PAGES
        return [pltpu.make_async_copy(self.ck_hbm.at[self.pt_ref[first + i]], self.kbuf.at[slot, i], self.sem.at[slot])
                for i in range(SCAN_REGION_PAGES)]

    def prologue(self):
        @pl.when(self.step == 0)
        def _():
            for g in range(SCAN_AHEAD):
                for c in self.copies(g, g):
                    c.start()

    def region(self, r):
        region = self.step * self.regions + r
        last_region = self.n_steps * self.regions - 1
        slot = r % SCAN_SLOTS
        for c in self.copies(jnp.minimum(region + SCAN_AHEAD, last_region), (r + SCAN_AHEAD) % SCAN_SLOTS):
            c.start()
        for c in self.copies(region, slot):
            c.wait()
        seq_local, part = divmod(r, REGIONS_PER_SEQ)
        if part == 0:
            n = self.seq_base + self.step * self.seqs + seq_local
            self.qb_ref[...] = jnp.broadcast_to(_column(self.qst_ref[...], n), (D_ATTN, LANES))
            self.gate = jnp.full((N_HEADS, LANES), -jnp.inf, F32)

        def store_row(h, i, row):
            pg = part * SCAN_REGION_PAGES + i
            self.lg_ref[seq_local, h, pg:pg + 1, :] = row

        self.gate = _scan_pages(lambda i: self.kbuf.at[slot, i], SCAN_REGION_PAGES, self.qb_ref, store_row, self.gate,
                                part * REGION_BLOCKS)
        if part == REGIONS_PER_SEQ - 1:
            self.idx_ref[seq_local] = _top_blocks(self.gate)

    def epilogue(self):
        @pl.when(self.step == self.n_steps - 1)
        def _():
            for slot in range(SCAN_AHEAD):
                for c in self.copies(self.n_steps * self.regions - 1, slot):
                    c.wait()

    def specs(self):
        in_specs = [pl.BlockSpec((D_ATTN, DEC_BATCH), lambda i, pt: (0, 0)), pl.BlockSpec(memory_space=pl.ANY)]
        out_specs = [pl.BlockSpec((self.seqs, N_HEADS, N_PAGES, PAGE_SIZE), lambda i, pt: (i, 0, 0, 0)),
                     pl.BlockSpec((self.seqs, N_HEADS, LANES), lambda i, pt: (i, 0, 0))]
        n_seqs = self.n_steps * self.seqs
        out_shapes = [jax.ShapeDtypeStruct((n_seqs, N_HEADS, N_PAGES, PAGE_SIZE), F32),
                      jax.ShapeDtypeStruct((n_seqs, N_HEADS, LANES), jnp.int32)]
        scratch = [pltpu.VMEM((SCAN_SLOTS, SCAN_REGION_PAGES, N_HEADS, HEAD_DIM, PAGE_SIZE), F32),
                   pltpu.VMEM((D_ATTN, LANES), F32),
                   pltpu.SemaphoreType.DMA((SCAN_SLOTS,))]
        return in_specs, out_specs, out_shapes, scratch


FFN_HOST = (FFN_SCAN_SEQS, FFN_STEPS, INPROJ_SEQS)
INPROJ_HOST = (INPROJ_SCAN_SEQS, INPROJ_STEPS, 0)


def _ffn_scan_kernel(pt_ref, x_ref, attn_ref, conv_ref, gt1_ref, sh2_ref, sc2_ref, gt2_ref, ga_ref, gffn_ref,
                     gfin_ref, wo_ref, w1_ref, w2_ref, qst_ref, ck_hbm, y_ref, lg_ref, idx_ref, kbuf, qb_ref, sem):
    scan = _ScanHost(*FFN_HOST)
    scan.bind(pt_ref, qst_ref, ck_hbm, lg_ref, idx_ref, kbuf, qb_ref, sem)
    scan.prologue()
    chunks_per_region = N_FF_CHUNKS // scan.regions
    x1 = h2 = acc = None
    for r in range(scan.regions):
        scan.region(r)
        if r == 0:
            x1, h2 = _ffn_head(x_ref, attn_ref, conv_ref, gt1_ref, sh2_ref, sc2_ref, ga_ref, gffn_ref, wo_ref)
        for c in range(r * chunks_per_region, (r + 1) * chunks_per_region):
            part = _ffn_chunk(h2, w1_ref, w2_ref, c)
            acc = part if acc is None else acc + part
    y_ref[...] = _rms(x1 + gt2_ref[...] * acc, gfin_ref[...])
    scan.epilogue()


def _ffn_scan(pt_flat, x2d, attn2d, conv2d, mod, mod_spec, g_attn, g_ffn, g_final, wo_b, w1_b, w2_b, qst, cache_kt):
    scan_in, scan_out, scan_shapes, scan_scratch = _ScanHost(*FFN_HOST).specs()
    grid_spec = pltpu.PrefetchScalarGridSpec(
        num_scalar_prefetch=1,
        grid=(FFN_STEPS,),
        in_specs=_ffn_in_specs(mod_spec, FFN_SCAN_TM) + scan_in,
        out_specs=[pl.BlockSpec((FFN_SCAN_TM, D_MODEL), lambda i, pt: (i, 0))] + scan_out,
        scratch_shapes=scan_scratch,
    )
    return pl.pallas_call(
        _ffn_scan_kernel,
        grid_spec=grid_spec,
        out_shape=[jax.ShapeDtypeStruct((BATCH * SEQ, D_MODEL), F32)] + scan_shapes,
        compiler_params=pltpu.CompilerParams(dimension_semantics=("arbitrary",), vmem_limit_bytes=VMEM_LIMIT),
        name="ffn_scan",
    )(pt_flat, x2d, attn2d, conv2d, mod, mod, mod, mod, g_attn, g_ffn, g_final, wo_b, w1_b, w2_b, qst, cache_kt)


INPROJ_TILES_PER_SEQ = SEQ // INPROJ_TM
PROJ_REGION = {1: 0, 3: 1, 5: 2}
assert INPROJ_TM % CONV_TM == 0 and INPROJ_TM // CONV_TM < INPROJ_SCAN_SEQS * REGIONS_PER_SEQ
assert max(PROJ_REGION) < INPROJ_SCAN_SEQS * REGIONS_PER_SEQ


def _inproj_scan_kernel(pt_ref, x_ref, sh_ref, sc_ref, g_ref, w_ref, wdw_ref, bdw_ref, lng_ref, lnb_ref, gco_ref,
                        qst_ref, ck_hbm, qt_ref, kt_ref, vt_ref, conv_ref, hist_ref, lg_ref, idx_ref,
                        ubuf, shift_ref, kbuf, qb_ref, sem):
    tm = INPROJ_TM
    scan = _ScanHost(*INPROJ_HOST)
    scan.bind(pt_ref, qst_ref, ck_hbm, lg_ref, idx_ref, kbuf, qb_ref, sem)
    scan.prologue()
    step = pl.program_id(0)

    @pl.when(step == 0)
    def _():
        ubuf[...] = jnp.zeros((CONV_HALO + tm, D_CONV), F32)

    hb = None

    def proj(i):
        return jnp.dot(hb, w_ref[:, i * D_ATTN:(i + 1) * D_ATTN], preferred_element_type=F32)

    for r in range(scan.regions):
        scan.region(r)
        if r == 0:
            hb = (_rms(x_ref[...], g_ref[...]) * (1.0 + sc_ref[...]) + sh_ref[...]).astype(BF16)
            u = proj(3) * jax.nn.sigmoid(proj(4))
            first_tile = lax.rem(step, INPROJ_TILES_PER_SEQ) == 0
            ubuf[0:CONV_HALO, :] = jnp.where(first_tile, 0.0, ubuf[tm:tm + CONV_HALO, :])
            ubuf[CONV_HALO:, :] = u
            hist_ref[...] = u[tm - CONV_HALO:, :]
        if r in PROJ_REGION:
            (qt_ref, kt_ref, vt_ref)[PROJ_REGION[r]][...] = proj(PROJ_REGION[r]).T
        if 1 <= r <= tm // CONV_TM:
            t0 = (r - 1) * CONV_TM
            conv = _conv_window(ubuf[t0:t0 + CONV_HALO + CONV_TM, :], wdw_ref, bdw_ref, lng_ref, lnb_ref, gco_ref,
                                shift_ref)
            conv_ref[t0:t0 + CONV_TM, :] = conv.astype(BF16)
    scan.epilogue()


def _inproj_scan(pt_flat, x2d, mod, mod_spec, g_mix, w_in_b, w_dw, b_dw, ln_g, ln_b, g_co, qst, cache_kt):
    tm = INPROJ_TM
    scan_in, scan_out, scan_shapes, scan_scratch = _ScanHost(*INPROJ_HOST).specs()
    t_spec = pl.BlockSpec((None, D_ATTN, tm),
                          lambda i, pt: (i // INPROJ_TILES_PER_SEQ, 0, i % INPROJ_TILES_PER_SEQ))
    t_out = jax.ShapeDtypeStruct((BATCH, D_ATTN, SEQ), F32)
    vec = pl.BlockSpec((1, D_CONV), lambda i, pt: (0, 0))
    grid_spec = pltpu.PrefetchScalarGridSpec(
        num_scalar_prefetch=1,
        grid=(INPROJ_STEPS,),
        in_specs=[
            pl.BlockSpec((tm, D_MODEL), lambda i, pt: (i, 0)),
            mod_spec(0),
            mod_spec(1),
            pl.BlockSpec((1, D_MODEL), lambda i, pt: (0, 0)),
            pl.BlockSpec((D_MODEL, D_IN), lambda i, pt: (0, 0), pipeline_mode=pl.Buffered(1)),
            pl.BlockSpec((CONV_WIDTH, D_CONV), lambda i, pt: (0, 0)),
            vec, vec, vec, vec,
        ] + scan_in,
        out_specs=[
            t_spec, t_spec, t_spec,
            pl.BlockSpec((tm, D_CONV), lambda i, pt: (i, 0)),
            pl.BlockSpec((None, CONV_HALO, D_CONV), lambda i, pt: (i // INPROJ_TILES_PER_SEQ, 0, 0)),
        ] + scan_out,
        scratch_shapes=[
            pltpu.VMEM((CONV_HALO + tm, D_CONV), F32),
            pltpu.VMEM((SUBLANES - 1, CONV_SPAN, D_CONV), F32),
        ] + scan_scratch,
    )
    return pl.pallas_call(
        _inproj_scan_kernel,
        grid_spec=grid_spec,
        out_shape=[t_out, t_out, t_out,
                   jax.ShapeDtypeStruct((BATCH * SEQ, D_CONV), BF16),
                   jax.ShapeDtypeStruct((BATCH, CONV_HALO, D_CONV), F32)] + scan_shapes,
        compiler_params=pltpu.CompilerParams(dimension_semantics=("arbitrary",), vmem_limit_bytes=VMEM_LIMIT),
        name="inproj_scan",
    )(pt_flat, x2d, mod, mod, g_mix, w_in_b, w_dw, b_dw, ln_g, ln_b, g_co, qst, cache_kt)


def kernel(x_prompt, x_sample, c_prompt, c_sample, cache_k, cache_v, state_conv, page_table, rel_bias, w_ada, b_ada, g_mix, w_in, w_dw, b_dw, ln_conv_g, ln_conv_b, g_attn_out, g_conv_out, w_out, g_ffn, w_ff1, w_ff2, g_final):
    w_in_b = w_in[0].astype(BF16)
    wo_b = w_out[0].astype(BF16)
    w1_b = w_ff1[0].astype(BF16)
    w2_b = w_ff2[0].astype(BF16)
    g_fin = g_final.reshape(1, D_MODEL)

    mod = _mod(jnp.concatenate([c_prompt, c_sample], axis=0), w_ada[0], b_ada)
    mod_p = mod[:BATCH].reshape(BATCH, 6, 1, D_MODEL)
    mod_s = mod[BATCH:]

    xs = x_sample.reshape(DEC_BATCH, D_MODEL)
    qst, kst, vst, us = _inproj(xs, mod_s, _mod_specs_sample, g_mix, w_in_b, DEC_BATCH, (D_ATTN, DEC_BATCH),
                                pl.BlockSpec((D_ATTN, DEC_BATCH), lambda i: (0, 0)))
    cache_kt = jnp.transpose(cache_k[0], (0, 2, 3, 1))
    cache_vt = jnp.transpose(cache_v[0], (0, 2, 3, 1))
    pt_flat = page_table.reshape(-1)

    xp = x_prompt.reshape(BATCH * SEQ, D_MODEL)
    qt, kt, vt, conv_n, u_tail, logits_a, sel_a = _inproj_scan(
        pt_flat, xp, mod_p, functools.partial(_mod_specs_prompt, INPROJ_TM), g_mix, w_in_b,
        w_dw[0], b_dw, ln_conv_g, ln_conv_b, g_conv_out, qst, cache_kt)
    attn = _moba_prompt(rel_bias, qt, kt, vt)
    y_p, logits_b, sel_b = _ffn_scan(
        pt_flat, xp, attn.reshape(BATCH * SEQ, D_ATTN), conv_n, mod_p,
        functools.partial(_mod_specs_prompt, FFN_SCAN_TM), g_attn_out, g_ffn, g_fin, wo_b, w1_b, w2_b, qst, cache_kt)

    hist_t = jnp.transpose(state_conv[0], (1, 0, 2))
    conv_s, new_hist_t = _conv_sample(hist_t, us, w_dw[0], b_dw, ln_conv_g, ln_conv_b, g_conv_out)
    sel_flat = jnp.concatenate([sel_a, sel_b], axis=0)[:, :, :MOBA_TOPK].reshape(-1)
    attn_st = _attn_sample(sel_flat, pt_flat, rel_bias, logits_a, logits_b, qst, kst, vst, cache_vt)
    y_s = _ffn(xs, attn_st.T, conv_s, mod_s, _mod_specs_sample,
               g_attn_out, g_ffn, g_fin, wo_b, w1_b, w2_b, DEC_BATCH)

    kv_p = lambda a: jnp.transpose(a.reshape(1, BATCH, N_HEADS, HEAD_DIM, SEQ), (0, 1, 4, 2, 3))
    kv_s = lambda a: jnp.transpose(a.reshape(1, 1, N_HEADS, HEAD_DIM, DEC_BATCH), (0, 4, 1, 2, 3))
    hist_p = u_tail[:, CONV_HALO - N_HIST:, :][None]
    hist_s = jnp.transpose(new_hist_t, (1, 0, 2))[None]
    return (y_p.reshape(BATCH, SEQ, D_MODEL), y_s.reshape(DEC_BATCH, 1, D_MODEL),
            kv_p(kt), kv_p(vt), hist_p, kv_s(kst), kv_s(vst), hist_s)
```

```python
import functools
import math

import numpy as np
import jax
import jax.numpy as jnp
from jax import lax
from jax.experimental import pallas as pl
from jax.experimental.pallas import tpu as pltpu

D_MODEL = 1024
BATCH = 8
SEQ = 2048
DEC_BATCH = 128
PAST_LEN = 8192
PAGE_SIZE = 128
D_ATTN = 512
D_CONV = 512
HEAD_DIM = 64
N_HEADS = 8
CONV_WIDTH = 31
MOBA_BLOCK = 256
MOBA_TOPK = 3
N_BUCKETS = 32
MAX_DISTANCE = 128
D_FF = 4096
EPS = 1e-6
D_IN = 3 * D_ATTN + 2 * D_CONV
N_PAGES = PAST_LEN // PAGE_SIZE
N_PAST_BLOCKS = PAST_LEN // MOBA_BLOCK
PAGES_PER_BLOCK = MOBA_BLOCK // PAGE_SIZE
N_PROMPT_BLOCKS = SEQ // MOBA_BLOCK
LANES = 128
SUBLANES = 8

F32 = jnp.float32
BF16 = jnp.bfloat16
NEG = -1e30
LOG2E = math.log2(math.e)
VMEM_LIMIT = 56 * 1024 * 1024


def _bucket_thresholds():
    n = np.arange(0, 4 * MAX_DISTANCE)
    max_exact = N_BUCKETS // 2
    ratio = np.maximum(n, max_exact).astype(np.float32) / np.float32(max_exact)
    val = np.log(ratio) / np.float32(math.log(MAX_DISTANCE / max_exact)) * np.float32(N_BUCKETS - max_exact)
    large = np.minimum(max_exact + val.astype(np.int32), N_BUCKETS - 1)
    bucket = np.where(n < max_exact, n, large)
    assert np.all(np.diff(bucket) >= 0) and bucket[-1] == N_BUCKETS - 1
    return [int(np.argmax(bucket >= b)) for b in range(N_BUCKETS)]


BUCKET_START = _bucket_thresholds()
FAR_DISTANCE = BUCKET_START[N_BUCKETS - 1]
assert FAR_DISTANCE <= MOBA_BLOCK


def _rms(x, g):
    return x * lax.rsqrt(jnp.mean(x * x, axis=-1, keepdims=True) + EPS) * g


def _bias_of_distance(dist, rb_ref, h):
    val = jnp.full(dist.shape, rb_ref[0, h], F32)
    for b in range(1, N_BUCKETS):
        val = jnp.where(dist >= BUCKET_START[b], rb_ref[b, h], val)
    return val


def _mod_kernel(c_ref, w_ref, b_ref, o_ref):
    c = c_ref[...]
    s = (c * jax.nn.sigmoid(c)).astype(BF16)
    o_ref[...] = jnp.dot(s, w_ref[...].astype(BF16), preferred_element_type=F32) + b_ref[...]


def _mod(c_all, w_ada, b_ada):
    rows = c_all.shape[0]
    bn = 1024
    return pl.pallas_call(
        _mod_kernel,
        grid=(6 * D_MODEL // bn,),
        in_specs=[
            pl.BlockSpec((rows, D_MODEL), lambda j: (0, 0)),
            pl.BlockSpec((D_MODEL, bn), lambda j: (0, j)),
            pl.BlockSpec((1, bn), lambda j: (0, j)),
        ],
        out_specs=pl.BlockSpec((rows, bn), lambda j: (0, j)),
        out_shape=jax.ShapeDtypeStruct((rows, 6 * D_MODEL), F32),
        compiler_params=pltpu.CompilerParams(dimension_semantics=("arbitrary",), vmem_limit_bytes=VMEM_LIMIT),
        name="mod",
    )(c_all, w_ada, b_ada)


def _mod_specs_prompt(tm, k):
    tiles_per_seq = SEQ // tm
    return pl.BlockSpec((None, None, 1, D_MODEL), lambda i, *_: (i // tiles_per_seq, k, 0, 0))


def _mod_specs_sample(k):
    return pl.BlockSpec((DEC_BATCH, D_MODEL), lambda i, *_: (0, k))


def _inproj_kernel(x_ref, sh_ref, sc_ref, g_ref, w_ref, qt_ref, kt_ref, vt_ref, u_ref):
    h = _rms(x_ref[...], g_ref[...]) * (1.0 + sc_ref[...]) + sh_ref[...]
    hb = h.astype(BF16)

    def proj(i):
        return jnp.dot(hb, w_ref[:, i * D_ATTN:(i + 1) * D_ATTN], preferred_element_type=F32)

    qt_ref[...] = proj(0).T
    kt_ref[...] = proj(1).T
    vt_ref[...] = proj(2).T
    a = proj(3)
    g = proj(4)
    u_ref[...] = a * jax.nn.sigmoid(g)


def _inproj(x2d, mod, mod_spec, g_mix, w_in_b, tm, t_shape, t_spec):
    rows = x2d.shape[0]
    row_spec = lambda width: pl.BlockSpec((tm, width), lambda i: (i, 0))
    t_out = jax.ShapeDtypeStruct(t_shape, F32)
    return pl.pallas_call(
        _inproj_kernel,
        grid=(rows // tm,),
        in_specs=[
            row_spec(D_MODEL),
            mod_spec(0),
            mod_spec(1),
            pl.BlockSpec((1, D_MODEL), lambda i: (0, 0)),
            pl.BlockSpec((D_MODEL, D_IN), lambda i: (0, 0)),
        ],
        out_specs=[t_spec, t_spec, t_spec, row_spec(D_CONV)],
        out_shape=[t_out, t_out, t_out, jax.ShapeDtypeStruct((rows, D_CONV), F32)],
        compiler_params=pltpu.CompilerParams(dimension_semantics=("arbitrary",), vmem_limit_bytes=VMEM_LIMIT),
        name="inproj",
    )(x2d, mod, mod, g_mix, w_in_b)


CONV_TM = 128
CONV_HALO = 32


def _conv_post(y, lng, lnb, gco):
    mu = jnp.mean(y, axis=-1, keepdims=True)
    yc = y - mu
    yn = yc * lax.rsqrt(jnp.mean(yc * yc, axis=-1, keepdims=True) + EPS) * lng + lnb
    s = yn * jax.nn.sigmoid(yn)
    return _rms(s, gco)


CONV_SPAN = CONV_HALO + CONV_TM - SUBLANES


def _conv_window(win, wdw_ref, bdw_ref, lng_ref, lnb_ref, gco_ref, sh_ref):
    first = CONV_HALO - (CONV_WIDTH - 1)
    acc = jnp.zeros((CONV_TM, D_CONV), F32)
    for r in range(SUBLANES):
        offs = [o for o in range(first, first + CONV_WIDTH) if o % SUBLANES == r]
        if r > 0:
            sh_ref[r - 1] = win[r:r + CONV_SPAN, :]
        for o in offs:
            a8 = o - r
            tap = win[a8:a8 + CONV_TM, :] if r == 0 else sh_ref[r - 1, a8:a8 + CONV_TM, :]
            acc = acc + tap * wdw_ref[o - first:o - first + 1, :]
    y = acc + bdw_ref[...]
    return _conv_post(y, lng_ref[...], lnb_ref[...], gco_ref[...])


SCONV_TN = 32
N_HIST = CONV_WIDTH - 1


def _sconv_kernel(hist_ref, u_ref, wdw_ref, bdw_ref, lng_ref, lnb_ref, gco_ref, o_ref, nh_ref):
    u = u_ref[...]
    y = u * wdw_ref[N_HIST:CONV_WIDTH, :] + bdw_ref[...]
    for w in range(N_HIST):
        y = y + hist_ref[w] * wdw_ref[w:w + 1, :]
    o_ref[...] = _conv_post(y, lng_ref[...], lnb_ref[...], gco_ref[...]).astype(BF16)
    for w in range(N_HIST - 1):
        nh_ref[w] = hist_ref[w + 1]
    nh_ref[N_HIST - 1] = u


def _conv_sample(hist_t, u, w_dw, b_dw, ln_g, ln_b, g_co):
    vec = pl.BlockSpec((1, D_CONV), lambda i: (0, 0))
    hist_spec = pl.BlockSpec((N_HIST, SCONV_TN, D_CONV), lambda i: (0, i, 0))
    return pl.pallas_call(
        _sconv_kernel,
        grid=(DEC_BATCH // SCONV_TN,),
        in_specs=[
            hist_spec,
            pl.BlockSpec((SCONV_TN, D_CONV), lambda i: (i, 0)),
            pl.BlockSpec((CONV_WIDTH, D_CONV), lambda i: (0, 0)),
            vec, vec, vec, vec,
        ],
        out_specs=[pl.BlockSpec((SCONV_TN, D_CONV), lambda i: (i, 0)), hist_spec],
        out_shape=[jax.ShapeDtypeStruct((DEC_BATCH, D_CONV), BF16),
                   jax.ShapeDtypeStruct((N_HIST, DEC_BATCH, D_CONV), F32)],
        compiler_params=pltpu.CompilerParams(dimension_semantics=("arbitrary",), vmem_limit_bytes=VMEM_LIMIT),
        name="conv_sample",
    )(hist_t, u, w_dw, b_dw, ln_g, ln_b, g_co)


PAIR = 2 * HEAD_DIM


def _moba_kernel(rb_ref, qt_ref, kt_ref, vt_ref, o_ref, kb_ref, vte_ref, mask_ref, bias_ref, s_ref, p_ref, ot_ref):
    b = pl.program_id(0)
    hp = pl.program_id(1)
    blk = MOBA_BLOCK
    nb = N_PROMPT_BLOCKS
    drow = lax.broadcasted_iota(jnp.int32, (PAIR, 1), 0)
    head_rows = [(drow >= e * HEAD_DIM) & (drow < (e + 1) * HEAD_DIM) for e in range(2)]

    @pl.when(b == 0)
    def _():
        kk = lax.broadcasted_iota(jnp.int32, (blk, blk), 0)
        qq = lax.broadcasted_iota(jnp.int32, (blk, blk), 1)
        d0 = qq - kk
        for e in range(2):
            h = 2 * hp + e
            bias_ref[h, 0] = jnp.where(d0 >= 0, _bias_of_distance(jnp.maximum(d0, 0), rb_ref, h) * LOG2E, NEG)
            bias_ref[h, 1] = _bias_of_distance(d0 + blk, rb_ref, h) * LOG2E

    km_cols = []
    for j in range(nb):
        ktj = kt_ref[:, j * blk:(j + 1) * blk]
        kb_ref[j * blk:(j + 1) * blk, :] = ktj.T.astype(BF16)
        km_cols.append(jnp.sum(ktj, axis=1, keepdims=True) * (1.0 / blk))
    vt = vt_ref[...]
    for e in range(2):
        vte_ref[e] = jnp.where(head_rows[e], vt, 1.0).astype(BF16)

    qt = qt_ref[...]
    qblk = lax.broadcasted_iota(jnp.int32, (1, SEQ), 1) // blk
    for e in range(2):
        h = 2 * hp + e
        far2 = rb_ref[N_BUCKETS - 1, h] * LOG2E
        qh = qt[e * HEAD_DIM:(e + 1) * HEAD_DIM, :]
        gates = [jnp.sum(qh * km_cols[j][e * HEAD_DIM:(e + 1) * HEAD_DIM, :], axis=0, keepdims=True)
                 for j in range(nb - 1)]
        gate = jnp.concatenate(gates + [jnp.zeros((1, SEQ), F32)], axis=0)
        row = lax.broadcasted_iota(jnp.int32, (nb, SEQ), 0)
        cnt = jnp.zeros((nb, SEQ), jnp.int32)
        for i in range(nb - 1):
            beats = jnp.where(gates[i] > gate, 1, jnp.where((gates[i] == gate) & (i < row), 1, 0))
            cnt = cnt + jnp.where(i < qblk, beats, 0)
        keep = jnp.where(row < qblk, jnp.where(cnt < MOBA_TOPK, 0.0, NEG), NEG)
        mask_ref[e] = keep + far2

    qscale = (HEAD_DIM ** -0.5) * LOG2E
    for qi in range(nb):
        qs = slice(qi * blk, (qi + 1) * blk)
        keys = (qi + 1) * blk
        qtile = qt_ref[:, qs] * qscale
        qcat = jnp.concatenate([jnp.where(head_rows[e], qtile, 0.0) for e in range(2)], axis=1).astype(BF16)
        s_ref[0:keys, :] = jnp.dot(kb_ref[0:keys, :], qcat, preferred_element_type=F32)
        shifts = []
        for e in range(2):
            h = 2 * hp + e
            far2 = rb_ref[N_BUCKETS - 1, h] * LOG2E
            cs = slice(e * blk, (e + 1) * blk)
            rows = []
            tops = []
            for j in range(qi + 1):
                ks = slice(j * blk, (j + 1) * blk)
                if j == qi:
                    sj = s_ref[ks, cs] + bias_ref[h, 0]
                    s_ref[ks, cs] = sj
                    row = None
                elif j == qi - 1:
                    sj = s_ref[ks, cs] + bias_ref[h, 1]
                    s_ref[ks, cs] = sj
                    row = mask_ref[e, j:j + 1, qs] - far2
                else:
                    sj = s_ref[ks, cs]
                    row = mask_ref[e, j:j + 1, qs]
                top = jnp.max(sj, axis=0, keepdims=True)
                rows.append(row)
                tops.append(top if row is None else top + row)
            m = functools.reduce(jnp.maximum, tops)
            shifts.append([m if row is None else m - row for row in rows])
        for e in range(2):
            cs = slice(e * blk, (e + 1) * blk)
            for j in range(qi + 1):
                ks = slice(j * blk, (j + 1) * blk)
                p_ref[ks, cs] = jnp.exp2(s_ref[ks, cs] - shifts[e][j]).astype(BF16)
        for e in range(2):
            cs = slice(e * blk, (e + 1) * blk)
            ot = jnp.dot(vte_ref[e, :, 0:keys], p_ref[0:keys, cs], preferred_element_type=F32)
            denom = ot[(1 - e) * HEAD_DIM:(1 - e) * HEAD_DIM + 1, :]
            ot_ref[e * HEAD_DIM:(e + 1) * HEAD_DIM, qs] = ot[e * HEAD_DIM:(e + 1) * HEAD_DIM, :] / denom
    o_ref[...] = ot_ref[...].T


def _moba_prompt(rel_bias, qt3, kt3, vt3):
    slab = pl.BlockSpec((None, PAIR, SEQ), lambda b, hp: (b, hp, 0))
    return pl.pallas_call(
        _moba_kernel,
        grid=(BATCH, N_HEADS // 2),
        in_specs=[pl.BlockSpec(memory_space=pltpu.SMEM), slab, slab, slab],
        out_specs=pl.BlockSpec((None, SEQ, PAIR), lambda b, hp: (b, 0, hp)),
        out_shape=jax.ShapeDtypeStruct((BATCH, SEQ, D_ATTN), F32),
        scratch_shapes=[
            pltpu.VMEM((SEQ, PAIR), BF16),
            pltpu.VMEM((2, PAIR, SEQ), BF16),
            pltpu.VMEM((2, N_PROMPT_BLOCKS, SEQ), F32),
            pltpu.VMEM((N_HEADS, 2, MOBA_BLOCK, MOBA_BLOCK), F32),
            pltpu.VMEM((SEQ, 2 * MOBA_BLOCK), F32),
            pltpu.VMEM((SEQ, 2 * MOBA_BLOCK), BF16),
            pltpu.VMEM((PAIR, SEQ), F32),
        ],
        compiler_params=pltpu.CompilerParams(dimension_semantics=("arbitrary", "arbitrary"),
                                             vmem_limit_bytes=VMEM_LIMIT),
        name="moba_prompt",
    )(rel_bias, qt3, kt3, vt3)


def _column(mat, n):
    lane = lax.broadcasted_iota(jnp.int32, mat.shape, 1)
    return jnp.sum(jnp.where(lane == n, mat, 0.0), axis=1, keepdims=True)


def _scan_pages(page, n_pages, qb_ref, store_row, g, first_block):
    lane = lax.broadcasted_iota(jnp.int32, (N_HEADS, LANES), 1)
    sub = lax.broadcasted_iota(jnp.int32, (N_HEADS, LANES), 0)
    for h in range(N_HEADS):
        qh = qb_ref[h * HEAD_DIM:(h + 1) * HEAD_DIM, :]
        for bi in range(n_pages // PAGES_PER_BLOCK):
            block_row = None
            for half in range(PAGES_PER_BLOCK):
                i = bi * PAGES_PER_BLOCK + half
                row = jnp.sum(page(i)[h] * qh, axis=0, keepdims=True)
                store_row(h, i, row)
                block_row = row if block_row is None else block_row + row
            gate_hb = jnp.sum(block_row, axis=1, keepdims=True) * (1.0 / MOBA_BLOCK)
            g = jnp.where((sub == h) & (lane == first_block + bi), gate_hb, g)
    return g


def _top_blocks(g):
    lane = lax.broadcasted_iota(jnp.int32, (N_HEADS, LANES), 1)
    lane_f = lane.astype(F32)
    out = jnp.zeros((N_HEADS, LANES), jnp.int32)
    for r in range(MOBA_TOPK):
        best = jnp.max(g, axis=1, keepdims=True)
        pick = jnp.min(jnp.where(g == best, lane_f, float(LANES)), axis=1, keepdims=True)
        out = jnp.where(lane == r, pick.astype(jnp.int32), out)
        g = jnp.where(lane_f == pick, -jnp.inf, g)
    return out


SLABS = MOBA_TOPK * PAGES_PER_BLOCK
SATTN_SLOTS = 3
assert SLABS + 1 <= SUBLANES


def _sattn_copies(sel_ref, pt_ref, cv_hbm, vbuf, sem, n, slot):
    out = []
    for h in range(N_HEADS):
        for t in range(MOBA_TOPK):
            blk = sel_ref[(n * N_HEADS + h) * MOBA_TOPK + t]
            for half in range(PAGES_PER_BLOCK):
                page = pt_ref[n * N_PAGES + blk * PAGES_PER_BLOCK + half]
                out.append(pltpu.make_async_copy(cv_hbm.at[page, h], vbuf.at[slot, h, t * PAGES_PER_BLOCK + half],
                                                 sem.at[slot]))
    return out


def _sattn_kernel(sel_ref, pt_ref, rb_ref, lga_ref, lgb_ref, qt_ref, kt_ref, vt_ref, cv_hbm, o_ref,
                  vbuf, tab_ref, self_ref, acc_ref, w_ref, sem):
    n = pl.program_id(0)
    slot = lax.rem(n, SATTN_SLOTS)
    copies = functools.partial(_sattn_copies, sel_ref, pt_ref, cv_hbm, vbuf, sem)
    scale = HEAD_DIM ** -0.5
    lane = lax.broadcasted_iota(jnp.int32, (SUBLANES, LANES), 1)
    sub = lax.broadcasted_iota(jnp.int32, (SUBLANES, LANES), 0)
    from_inproj = n < INPROJ_SEQS

    @pl.when(n == 0)
    def _():
        for ahead in range(SATTN_SLOTS - 1):
            for c in copies(ahead, ahead):
                c.start()
        dist = MOBA_BLOCK - lax.broadcasted_iota(jnp.int32, (1, MOBA_BLOCK), 1)
        prod = qt_ref[...] * kt_ref[...]
        for h in range(N_HEADS):
            tab_ref[h:h + 1, :] = _bias_of_distance(dist, rb_ref, h)
            self_ref[h:h + 1, :] = (jnp.sum(prod[h * HEAD_DIM:(h + 1) * HEAD_DIM, :], axis=0, keepdims=True) * scale
                                    + rb_ref[0, h])
        w_ref[...] = jnp.zeros((N_HEADS, LANES), F32)
        acc_ref[...] = jnp.zeros((D_ATTN, DEC_BATCH), F32)

    @pl.when(n + SATTN_SLOTS - 1 < DEC_BATCH)
    def _():
        for c in copies(n + SATTN_SLOTS - 1, lax.rem(n + SATTN_SLOTS - 1, SATTN_SLOTS)):
            c.start()

    for c in copies(n, slot):
        c.wait()

    self_col = _column(self_ref[...], n)
    w_tile = w_ref[...]
    lane_wide = lax.broadcasted_iota(jnp.int32, (HEAD_DIM, LANES), 1)
    for h in range(N_HEADS):
        far_bias = rb_ref[N_BUCKETS - 1, h]
        s = jnp.full((SUBLANES, LANES), NEG, F32)
        for t in range(MOBA_TOPK):
            blk = sel_ref[(n * N_HEADS + h) * MOBA_TOPK + t]
            for half in range(PAGES_PER_BLOCK):
                r = t * PAGES_PER_BLOCK + half
                pg = pl.ds(blk * PAGES_PER_BLOCK + half, 1)
                row = jnp.where(from_inproj, lga_ref[h, pg, :], lgb_ref[h, pg, :]) * scale
                bias = jnp.where(blk == N_PAST_BLOCKS - 1, tab_ref[h:h + 1, half * PAGE_SIZE:(half + 1) * PAGE_SIZE],
                                 far_bias)
                s = jnp.where(sub == r, row + bias, s)
        s_new = self_col[h:h + 1, :]
        m = jnp.maximum(jnp.max(jnp.max(s, axis=1, keepdims=True), axis=0, keepdims=True), s_new)
        p = jnp.exp(s - m)
        p_new = jnp.exp(s_new - m)
        l = jnp.sum(jnp.sum(p, axis=1, keepdims=True), axis=0, keepdims=True) + p_new
        acc = jnp.zeros((HEAD_DIM, LANES), F32)
        for r in range(SLABS):
            acc = acc + vbuf[slot, h, r] * p[r:r + 1, :]
        col = jnp.sum(acc, axis=1, keepdims=True) / l
        rows = slice(h * HEAD_DIM, (h + 1) * HEAD_DIM)
        acc_ref[rows, :] = jnp.where(lane_wide == n, col, acc_ref[rows, :])
        w_tile = jnp.where((sub == h) & (lane == n), p_new / l, w_tile)
    w_ref[...] = w_tile

    @pl.when(n == DEC_BATCH - 1)
    def _():
        for h in range(N_HEADS):
            rows = slice(h * HEAD_DIM, (h + 1) * HEAD_DIM)
            o_ref[rows, :] = acc_ref[rows, :] + w_tile[h:h + 1, :] * vt_ref[rows, :]


def _attn_sample(sel_flat, pt_flat, rel_bias, logits_a, logits_b, qst, kst, vst, cache_vt):
    full = pl.BlockSpec((D_ATTN, DEC_BATCH), lambda n, sel, pt: (0, 0))
    seq_logits = (None, N_HEADS, N_PAGES, PAGE_SIZE)
    n_a = logits_a.shape[0]
    assert n_a == INPROJ_SEQS
    grid_spec = pltpu.PrefetchScalarGridSpec(
        num_scalar_prefetch=2,
        grid=(DEC_BATCH,),
        in_specs=[
            pl.BlockSpec(memory_space=pltpu.SMEM),
            pl.BlockSpec(seq_logits, lambda n, sel, pt: (jnp.minimum(n, n_a - 1), 0, 0, 0)),
            pl.BlockSpec(seq_logits, lambda n, sel, pt: (jnp.maximum(n - n_a, 0), 0, 0, 0)),
            full, full, full,
            pl.BlockSpec(memory_space=pl.ANY),
        ],
        out_specs=full,
        scratch_shapes=[
            pltpu.VMEM((SATTN_SLOTS, N_HEADS, SLABS, HEAD_DIM, PAGE_SIZE), F32),
            pltpu.VMEM((N_HEADS, MOBA_BLOCK), F32),
            pltpu.VMEM((N_HEADS, DEC_BATCH), F32),
            pltpu.VMEM((D_ATTN, DEC_BATCH), F32),
            pltpu.VMEM((N_HEADS, DEC_BATCH), F32),
            pltpu.SemaphoreType.DMA((SATTN_SLOTS,)),
        ],
    )
    return pl.pallas_call(
        _sattn_kernel,
        grid_spec=grid_spec,
        out_shape=jax.ShapeDtypeStruct((D_ATTN, DEC_BATCH), F32),
        compiler_params=pltpu.CompilerParams(dimension_semantics=("arbitrary",), vmem_limit_bytes=VMEM_LIMIT),
        name="attn_sample",
    )(sel_flat, pt_flat, rel_bias, logits_a, logits_b, qst, kst, vst, cache_vt)


FF_CHUNK = 1024
N_FF_CHUNKS = D_FF // FF_CHUNK


def _ffn_head(x_ref, attn_ref, conv_ref, gt1_ref, sh2_ref, sc2_ref, ga_ref, gffn_ref, wo_ref):
    attn_n = _rms(attn_ref[...], ga_ref[...]).astype(BF16)
    mixed = (jnp.dot(attn_n, wo_ref[:D_ATTN, :], preferred_element_type=F32)
             + jnp.dot(conv_ref[...], wo_ref[D_ATTN:, :], preferred_element_type=F32))
    x1 = x_ref[...] + gt1_ref[...] * mixed
    h2 = (_rms(x1, gffn_ref[...]) * (1.0 + sc2_ref[...]) + sh2_ref[...]).astype(BF16)
    return x1, h2


def _ffn_chunk(h2, w1_ref, w2_ref, c):
    f = jnp.dot(h2, w1_ref[:, c * FF_CHUNK:(c + 1) * FF_CHUNK], preferred_element_type=F32)
    f = jnp.square(jnp.maximum(f, 0.0)).astype(BF16)
    return jnp.dot(f, w2_ref[c * FF_CHUNK:(c + 1) * FF_CHUNK, :], preferred_element_type=F32)


def _ffn_kernel(x_ref, attn_ref, conv_ref, gt1_ref, sh2_ref, sc2_ref, gt2_ref, ga_ref, gffn_ref, gfin_ref,
                wo_ref, w1_ref, w2_ref, y_ref):
    x1, h2 = _ffn_head(x_ref, attn_ref, conv_ref, gt1_ref, sh2_ref, sc2_ref, ga_ref, gffn_ref, wo_ref)
    acc = _ffn_chunk(h2, w1_ref, w2_ref, 0)
    for c in range(1, N_FF_CHUNKS):
        acc = acc + _ffn_chunk(h2, w1_ref, w2_ref, c)
    y_ref[...] = _rms(x1 + gt2_ref[...] * acc, gfin_ref[...])


def _ffn_in_specs(mod_spec, tm):
    row_spec = lambda width: pl.BlockSpec((tm, width), lambda i, *_: (i, 0))
    const = lambda shape: pl.BlockSpec(shape, lambda i, *_: (0, 0), pipeline_mode=pl.Buffered(1))
    return [
        row_spec(D_MODEL), row_spec(D_ATTN), row_spec(D_CONV),
        mod_spec(2), mod_spec(3), mod_spec(4), mod_spec(5),
        const((1, D_ATTN)), const((1, D_MODEL)), const((1, D_MODEL)),
        const((D_MODEL, D_MODEL)), const((D_MODEL, D_FF)), const((D_FF, D_MODEL)),
    ]


def _ffn(x2d, attn2d, conv2d, mod, mod_spec, g_attn, g_ffn, g_final, wo_b, w1_b, w2_b, tm):
    rows = x2d.shape[0]
    return pl.pallas_call(
        _ffn_kernel,
        grid=(rows // tm,),
        in_specs=_ffn_in_specs(mod_spec, tm),
        out_specs=pl.BlockSpec((tm, D_MODEL), lambda i: (i, 0)),
        out_shape=jax.ShapeDtypeStruct((rows, D_MODEL), F32),
        compiler_params=pltpu.CompilerParams(dimension_semantics=("arbitrary",), vmem_limit_bytes=VMEM_LIMIT),
        name="ffn",
    )(x2d, attn2d, conv2d, mod, mod, mod, mod, g_attn, g_ffn, g_final, wo_b, w1_b, w2_b)


SCAN_REGION_PAGES = 16
SCAN_SLOTS = 4
SCAN_AHEAD = SCAN_SLOTS - 1
REGIONS_PER_SEQ = N_PAGES // SCAN_REGION_PAGES
REGION_BLOCKS = SCAN_REGION_PAGES // PAGES_PER_BLOCK
INPROJ_TM = 512
INPROJ_SCAN_SEQS = 2
FFN_SCAN_TM = 256
FFN_SCAN_SEQS = 1
INPROJ_STEPS = BATCH * SEQ // INPROJ_TM
FFN_STEPS = BATCH * SEQ // FFN_SCAN_TM
INPROJ_SEQS = INPROJ_STEPS * INPROJ_SCAN_SEQS
assert INPROJ_SEQS + FFN_STEPS * FFN_SCAN_SEQS == DEC_BATCH


class _ScanHost:
    def __init__(self, seqs, n_steps, seq_base):
        self.regions = seqs * REGIONS_PER_SEQ
        self.seqs = seqs
        self.n_steps = n_steps
        self.seq_base = seq_base
        assert self.regions % SCAN_SLOTS == 0
        assert n_steps * self.regions > SCAN_AHEAD

    def bind(self, pt_ref, qst_ref, ck_hbm, lg_ref, idx_ref, kbuf, qb_ref, sem):
        self.pt_ref, self.qst_ref, self.ck_hbm = pt_ref, qst_ref, ck_hbm
        self.lg_ref, self.idx_ref, self.kbuf, self.qb_ref, self.sem = lg_ref, idx_ref, kbuf, qb_ref, sem
        self.step = pl.program_id(0)
        self.gate = None

    def copies(self, region, slot):
        first = self.seq_base * N_PAGES + region * SCAN_REGION_PAGES
        return [pltpu.make_async_copy(self.ck_hbm.at[self.pt_ref[first + i]], self.kbuf.at[slot, i], self.sem.at[slot])
                for i in range(SCAN_REGION_PAGES)]

    def prologue(self):
        @pl.when(self.step == 0)
        def _():
            for g in range(SCAN_AHEAD):
                for c in self.copies(g, g):
                    c.start()

    def region(self, r):
        region = self.step * self.regions + r
        last_region = self.n_steps * self.regions - 1
        slot = r % SCAN_SLOTS
        for c in self.copies(jnp.minimum(region + SCAN_AHEAD, last_region), (r + SCAN_AHEAD) % SCAN_SLOTS):
            c.start()
        for c in self.copies(region, slot):
            c.wait()
        seq_local, part = divmod(r, REGIONS_PER_SEQ)
        if part == 0:
            n = self.seq_base + self.step * self.seqs + seq_local
            self.qb_ref[...] = jnp.broadcast_to(_column(self.qst_ref[...], n), (D_ATTN, LANES))
            self.gate = jnp.full((N_HEADS, LANES), -jnp.inf, F32)

        def store_row(h, i, row):
            pg = part * SCAN_REGION_PAGES + i
            self.lg_ref[seq_local, h, pg:pg + 1, :] = row

        self.gate = _scan_pages(lambda i: self.kbuf.at[slot, i], SCAN_REGION_PAGES, self.qb_ref, store_row, self.gate,
                                part * REGION_BLOCKS)
        if part == REGIONS_PER_SEQ - 1:
            self.idx_ref[seq_local] = _top_blocks(self.gate)

    def epilogue(self):
        @pl.when(self.step == self.n_steps - 1)
        def _():
            for slot in range(SCAN_AHEAD):
                for c in self.copies(self.n_steps * self.regions - 1, slot):
                    c.wait()

    def specs(self):
        in_specs = [pl.BlockSpec((D_ATTN, DEC_BATCH), lambda i, pt: (0, 0)), pl.BlockSpec(memory_space=pl.ANY)]
        out_specs = [pl.BlockSpec((self.seqs, N_HEADS, N_PAGES, PAGE_SIZE), lambda i, pt: (i, 0, 0, 0)),
                     pl.BlockSpec((self.seqs, N_HEADS, LANES), lambda i, pt: (i, 0, 0))]
        n_seqs = self.n_steps * self.seqs
        out_shapes = [jax.ShapeDtypeStruct((n_seqs, N_HEADS, N_PAGES, PAGE_SIZE), F32),
                      jax.ShapeDtypeStruct((n_seqs, N_HEADS, LANES), jnp.int32)]
        scratch = [pltpu.VMEM((SCAN_SLOTS, SCAN_REGION_PAGES, N_HEADS, HEAD_DIM, PAGE_SIZE), F32),
                   pltpu.VMEM((D_ATTN, LANES), F32),
                   pltpu.SemaphoreType.DMA((SCAN_SLOTS,))]
        return in_specs, out_specs, out_shapes, scratch


FFN_HOST = (FFN_SCAN_SEQS, FFN_STEPS, INPROJ_SEQS)
INPROJ_HOST = (INPROJ_SCAN_SEQS, INPROJ_STEPS, 0)


def _ffn_scan_kernel(pt_ref, x_ref, attn_ref, conv_ref, gt1_ref, sh2_ref, sc2_ref, gt2_ref, ga_ref, gffn_ref,
                     gfin_ref, wo_ref, w1_ref, w2_ref, qst_ref, ck_hbm, y_ref, lg_ref, idx_ref, kbuf, qb_ref, sem):
    scan = _ScanHost(*FFN_HOST)
    scan.bind(pt_ref, qst_ref, ck_hbm, lg_ref, idx_ref, kbuf, qb_ref, sem)
    scan.prologue()
    chunks_per_region = N_FF_CHUNKS // scan.regions
    x1 = h2 = acc = None
    for r in range(scan.regions):
        scan.region(r)
        if r == 0:
            x1, h2 = _ffn_head(x_ref, attn_ref, conv_ref, gt1_ref, sh2_ref, sc2_ref, ga_ref, gffn_ref, wo_ref)
        for c in range(r * chunks_per_region, (r + 1) * chunks_per_region):
            part = _ffn_chunk(h2, w1_ref, w2_ref, c)
            acc = part if acc is None else acc + part
    y_ref[...] = _rms(x1 + gt2_ref[...] * acc, gfin_ref[...])
    scan.epilogue()


def _ffn_scan(pt_flat, x2d, attn2d, conv2d, mod, mod_spec, g_attn, g_ffn, g_final, wo_b, w1_b, w2_b, qst, cache_kt):
    scan_in, scan_out, scan_shapes, scan_scratch = _ScanHost(*FFN_HOST).specs()
    grid_spec = pltpu.PrefetchScalarGridSpec(
        num_scalar_prefetch=1,
        grid=(FFN_STEPS,),
        in_specs=_ffn_in_specs(mod_spec, FFN_SCAN_TM) + scan_in,
        out_specs=[pl.BlockSpec((FFN_SCAN_TM, D_MODEL), lambda i, pt: (i, 0))] + scan_out,
        scratch_shapes=scan_scratch,
    )
    return pl.pallas_call(
        _ffn_scan_kernel,
        grid_spec=grid_spec,
        out_shape=[jax.ShapeDtypeStruct((BATCH * SEQ, D_MODEL), F32)] + scan_shapes,
        compiler_params=pltpu.CompilerParams(dimension_semantics=("arbitrary",), vmem_limit_bytes=VMEM_LIMIT),
        name="ffn_scan",
    )(pt_flat, x2d, attn2d, conv2d, mod, mod, mod, mod, g_attn, g_ffn, g_final, wo_b, w1_b, w2_b, qst, cache_kt)


INPROJ_TILES_PER_SEQ = SEQ // INPROJ_TM
PROJ_REGION = {1: 0, 3: 1, 5: 2}
assert INPROJ_TM % CONV_TM == 0 and INPROJ_TM // CONV_TM < INPROJ_SCAN_SEQS * REGIONS_PER_SEQ
assert max(PROJ_REGION) < INPROJ_SCAN_SEQS * REGIONS_PER_SEQ


def _inproj_scan_kernel(pt_ref, x_ref, sh_ref, sc_ref, g_ref, w_ref, wdw_ref, bdw_ref, lng_ref, lnb_ref, gco_ref,
                        qst_ref, ck_hbm, qt_ref, kt_ref, vt_ref, conv_ref, hist_ref, lg_ref, idx_ref,
                        ubuf, shift_ref, kbuf, qb_ref, sem):
    tm = INPROJ_TM
    scan = _ScanHost(*INPROJ_HOST)
    scan.bind(pt_ref, qst_ref, ck_hbm, lg_ref, idx_ref, kbuf, qb_ref, sem)
    scan.prologue()
    step = pl.program_id(0)

    @pl.when(step == 0)
    def _():
        ubuf[...] = jnp.zeros((CONV_HALO + tm, D_CONV), F32)

    hb = None

    def proj(i):
        return jnp.dot(hb, w_ref[:, i * D_ATTN:(i + 1) * D_ATTN], preferred_element_type=F32)

    for r in range(scan.regions):
        scan.region(r)
        if r == 0:
            hb = (_rms(x_ref[...], g_ref[...]) * (1.0 + sc_ref[...]) + sh_ref[...]).astype(BF16)
            u = proj(3) * jax.nn.sigmoid(proj(4))
            first_tile = lax.rem(step, INPROJ_TILES_PER_SEQ) == 0
            ubuf[0:CONV_HALO, :] = jnp.where(first_tile, 0.0, ubuf[tm:tm + CONV_HALO, :])
            ubuf[CONV_HALO:, :] = u
            hist_ref[...] = u[tm - CONV_HALO:, :]
        if r in PROJ_REGION:
            (qt_ref, kt_ref, vt_ref)[PROJ_REGION[r]][...] = proj(PROJ_REGION[r]).T
        if 1 <= r <= tm // CONV_TM:
            t0 = (r - 1) * CONV_TM
            conv = _conv_window(ubuf[t0:t0 + CONV_HALO + CONV_TM, :], wdw_ref, bdw_ref, lng_ref, lnb_ref, gco_ref,
                                shift_ref)
            conv_ref[t0:t0 + CONV_TM, :] = conv.astype(BF16)
    scan.epilogue()


def _inproj_scan(pt_flat, x2d, mod, mod_spec, g_mix, w_in_b, w_dw, b_dw, ln_g, ln_b, g_co, qst, cache_kt):
    tm = INPROJ_TM
    scan_in, scan_out, scan_shapes, scan_scratch = _ScanHost(*INPROJ_HOST).specs()
    t_spec = pl.BlockSpec((None, D_ATTN, tm),
                          lambda i, pt: (i // INPROJ_TILES_PER_SEQ, 0, i % INPROJ_TILES_PER_SEQ))
    t_out = jax.ShapeDtypeStruct((BATCH, D_ATTN, SEQ), F32)
    vec = pl.BlockSpec((1, D_CONV), lambda i, pt: (0, 0))
    grid_spec = pltpu.PrefetchScalarGridSpec(
        num_scalar_prefetch=1,
        grid=(INPROJ_STEPS,),
        in_specs=[
            pl.BlockSpec((tm, D_MODEL), lambda i, pt: (i, 0)),
            mod_spec(0),
            mod_spec(1),
            pl.BlockSpec((1, D_MODEL), lambda i, pt: (0, 0)),
            pl.BlockSpec((D_MODEL, D_IN), lambda i, pt: (0, 0), pipeline_mode=pl.Buffered(1)),
            pl.BlockSpec((CONV_WIDTH, D_CONV), lambda i, pt: (0, 0)),
            vec, vec, vec, vec,
        ] + scan_in,
        out_specs=[
            t_spec, t_spec, t_spec,
            pl.BlockSpec((tm, D_CONV), lambda i, pt: (i, 0)),
            pl.BlockSpec((None, CONV_HALO, D_CONV), lambda i, pt: (i // INPROJ_TILES_PER_SEQ, 0, 0)),
        ] + scan_out,
        scratch_shapes=[
            pltpu.VMEM((CONV_HALO + tm, D_CONV), F32),
            pltpu.VMEM((SUBLANES - 1, CONV_SPAN, D_CONV), F32),
        ] + scan_scratch,
    )
    return pl.pallas_call(
        _inproj_scan_kernel,
        grid_spec=grid_spec,
        out_shape=[t_out, t_out, t_out,
                   jax.ShapeDtypeStruct((BATCH * SEQ, D_CONV), BF16),
                   jax.ShapeDtypeStruct((BATCH, CONV_HALO, D_CONV), F32)] + scan_shapes,
        compiler_params=pltpu.CompilerParams(dimension_semantics=("arbitrary",), vmem_limit_bytes=VMEM_LIMIT),
        name="inproj_scan",
    )(pt_flat, x2d, mod, mod, g_mix, w_in_b, w_dw, b_dw, ln_g, ln_b, g_co, qst, cache_kt)


def kernel(x_prompt, x_sample, c_prompt, c_sample, cache_k, cache_v, state_conv, page_table, rel_bias, w_ada, b_ada, g_mix, w_in, w_dw, b_dw, ln_conv_g, ln_conv_b, g_attn_out, g_conv_out, w_out, g_ffn, w_ff1, w_ff2, g_final):
    w_in_b = w_in[0].astype(BF16)
    wo_b = w_out[0].astype(BF16)
    w1_b = w_ff1[0].astype(BF16)
    w2_b = w_ff2[0].astype(BF16)
    g_fin = g_final.reshape(1, D_MODEL)

    mod = _mod(jnp.concatenate([c_prompt, c_sample], axis=0), w_ada[0], b_ada)
    mod_p = mod[:BATCH].reshape(BATCH, 6, 1, D_MODEL)
    mod_s = mod[BATCH:]

    xs = x_sample.reshape(DEC_BATCH, D_MODEL)
    qst, kst, vst, us = _inproj(xs, mod_s, _mod_specs_sample, g_mix, w_in_b, DEC_BATCH, (D_ATTN, DEC_BATCH),
                                pl.BlockSpec((D_ATTN, DEC_BATCH), lambda i: (0, 0)))
    cache_kt = jnp.transpose(cache_k[0], (0, 2, 3, 1))
    cache_vt = jnp.transpose(cache_v[0], (0, 2, 3, 1))
    pt_flat = page_table.reshape(-1)

    xp = x_prompt.reshape(BATCH * SEQ, D_MODEL)
    qt, kt, vt, conv_n, u_tail, logits_a, sel_a = _inproj_scan(
        pt_flat, xp, mod_p, functools.partial(_mod_specs_prompt, INPROJ_TM), g_mix, w_in_b,
        w_dw[0], b_dw, ln_conv_g, ln_conv_b, g_conv_out, qst, cache_kt)
    attn = _moba_prompt(rel_bias, qt, kt, vt)
    y_p, logits_b, sel_b = _ffn_scan(
        pt_flat, xp, attn.reshape(BATCH * SEQ, D_ATTN), conv_n, mod_p,
        functools.partial(_mod_specs_prompt, FFN_SCAN_TM), g_attn_out, g_ffn, g_fin, wo_b, w1_b, w2_b, qst, cache_kt)

    hist_t = jnp.transpose(state_conv[0], (1, 0, 2))
    conv_s, new_hist_t = _conv_sample(hist_t, us, w_dw[0], b_dw, ln_conv_g, ln_conv_b, g_conv_out)
    sel_flat = jnp.concatenate([sel_a, sel_b], axis=0)[:, :, :MOBA_TOPK].reshape(-1)
    attn_st = _attn_sample(sel_flat, pt_flat, rel_bias, logits_a, logits_b, qst, kst, vst, cache_vt)
    y_s = _ffn(xs, attn_st.T, conv_s, mod_s, _mod_specs_sample,
               g_attn_out, g_ffn, g_fin, wo_b, w1_b, w2_b, DEC_BATCH)

    kv_p = lambda a: jnp.transpose(a.reshape(1, BATCH, N_HEADS, HEAD_DIM, SEQ), (0, 1, 4, 2, 3))
    kv_s = lambda a: jnp.transpose(a.reshape(1, 1, N_HEADS, HEAD_DIM, DEC_BATCH), (0, 4, 1, 2, 3))
    hist_p = u_tail[:, CONV_HALO - N_HIST:, :][None]
    hist_s = jnp.transpose(new_hist_t, (1, 0, 2))[None]
    return (y_p.reshape(BATCH, SEQ, D_MODEL), y_s.reshape(DEC_BATCH, 1, D_MODEL),
            kv_p(kt), kv_p(vt), hist_p, kv_s(kst), kv_s(vst), hist_s)
```
